```python
import numpy as np
import jax
import jax.numpy as jnp
from jax import lax

D_MODEL = 4096
BATCH = 2
SEQ = 8192
DEPTH = 4

CTX_LEN = 256
GRID_W = 64
HEAD_DIM = 128
MIX_WIDTH = D_MODEL
W_CONV = MIX_WIDTH // 4
W_NA = 3 * MIX_WIDTH // 8
W_HG = MIX_WIDTH - W_CONV - W_NA
N_NA_HEADS = W_NA // HEAD_DIM
N_HG_HEADS = W_HG // HEAD_DIM
IN_CONV = 2 * W_CONV
IN_NA = 3 * W_NA
IN_HG = 5 * W_HG
IN_WIDTH = IN_CONV + IN_NA + IN_HG
CONV_K = 31
NA_ROWS = 8
NA_COLS = 16
ROPE_THETA = 10000.0
HG_CHUNK = 64
F_MIN = 1e-6
N_EXPERTS = 16
EC_CAPACITY = 2
D_EXPERT = 512
MOD_RANK = 256
N_MOD = 6
NORM_EPS = 1e-6

kernel_name = 'hybrid_conv_na_hgrn2_ecmoe_dit'


def rms_norm(x, g):
    xf = x.astype(jnp.float32)
    y = xf * lax.rsqrt(jnp.mean(xf * xf, axis=-1, keepdims=True) + NORM_EPS)
    return (y * g.astype(jnp.float32)).astype(x.dtype)


def split_heads(a, j, width):
    B, T, _ = a.shape
    return a[..., j * width:(j + 1) * width].reshape(B, T, width // HEAD_DIM, HEAD_DIM)


def axial_rope(x, row, col):
    half = HEAD_DIM // 2
    n_freq = half // 2
    inv_freq = ROPE_THETA ** (-jnp.arange(n_freq, dtype=jnp.float32) / n_freq)

    def rotate(xp, pos):
        ang = pos.astype(jnp.float32)[:, None] * inv_freq[None, :]
        cos = jnp.cos(ang)[None, :, None, :]
        sin = jnp.sin(ang)[None, :, None, :]
        x1 = xp[..., :n_freq].astype(jnp.float32)
        x2 = xp[..., n_freq:].astype(jnp.float32)
        return jnp.concatenate([x1 * cos - x2 * sin, x1 * sin + x2 * cos], axis=-1)

    out = jnp.concatenate([rotate(x[..., :half], row), rotate(x[..., half:], col)], axis=-1)
    return out.astype(x.dtype)


def conformer_conv(a, w_dw, b_dw, ln_g, ln_b):
    u = a[..., :W_CONV] * jax.nn.sigmoid(a[..., W_CONV:])
    u = lax.conv_general_dilated(
        u, w_dw[:, None, :].astype(u.dtype), window_strides=(1,),
        padding=[(CONV_K // 2, CONV_K // 2)], dimension_numbers=('NWC', 'WIO', 'NWC'),
        feature_group_count=W_CONV) + b_dw
    uf = u.astype(jnp.float32)
    mu = jnp.mean(uf, axis=-1, keepdims=True)
    var = jnp.mean(jnp.square(uf - mu), axis=-1, keepdims=True)
    un = (uf - mu) * lax.rsqrt(var + NORM_EPS) * ln_g + ln_b
    return jax.nn.silu(un).astype(a.dtype)


def context_attention(q, k, v):
    B, M, H, Dh = q.shape
    s = jnp.einsum('bqhd,bkhd->bhqk', q, k).astype(jnp.float32) * Dh ** -0.5
    p = jax.nn.softmax(s, axis=-1).astype(v.dtype)
    return jnp.einsum('bhqk,bkhd->bqhd', p, v).reshape(B, M, H * Dh)


def neighbourhood_attention(q, k, v, k_ctx, v_ctx, rpb, rows):
    B, N, H, Dh = q.shape
    t = jnp.arange(N)
    row, col = t // GRID_W, t % GRID_W
    q_rot = axial_rope(q, row, col)
    k_rot = axial_rope(k, row, col)
    kr = min(NA_ROWS, rows)
    scale = Dh ** -0.5
    col_start = np.clip(np.arange(GRID_W) - NA_COLS // 2, 0, GRID_W - NA_COLS)
    col_idx = col_start[:, None] + np.arange(NA_COLS)[None, :]
    col_bias_idx = jnp.asarray(col_idx - np.arange(GRID_W)[:, None] + NA_COLS - 1)
    col_idx = jnp.asarray(col_idx)
    k_grid = k_rot.reshape(B, rows, GRID_W, H, Dh)
    v_grid = v.reshape(B, rows, GRID_W, H, Dh)
    q_rows = q_rot.reshape(B, rows, GRID_W, H, Dh).transpose(1, 0, 2, 3, 4)
    qp_rows = q.reshape(B, rows, GRID_W, H, Dh).transpose(1, 0, 2, 3, 4)
    n_win = kr * NA_COLS

    def one_row(args):
        qr, qp, r = args
        rs = jnp.clip(r - kr // 2, 0, rows - kr)
        kw = lax.dynamic_slice_in_dim(k_grid, rs, kr, axis=1)[:, :, col_idx]
        vw = lax.dynamic_slice_in_dim(v_grid, rs, kr, axis=1)[:, :, col_idx]
        row_bias_idx = rs + jnp.arange(kr) - r + NA_ROWS - 1
        bias = rpb[:, row_bias_idx[None, :, None], col_bias_idx[:, None, :]]
        s_lat = jnp.einsum('bqhd,brqjhd->bhqrj', qr, kw).astype(jnp.float32) * scale + bias[None]
        s_ctx = jnp.einsum('bqhd,bmhd->bhqm', qp, k_ctx).astype(jnp.float32) * scale
        s = jnp.concatenate([s_lat.reshape(B, H, GRID_W, n_win), s_ctx], axis=-1)
        p = jax.nn.softmax(s, axis=-1).astype(v.dtype)
        p_lat = p[..., :n_win].reshape(B, H, GRID_W, kr, NA_COLS)
        return (jnp.einsum('bhqrj,brqjhd->bqhd', p_lat, vw)
                + jnp.einsum('bhqm,bmhd->bqhd', p[..., n_win:], v_ctx))

    o = lax.map(one_row, (q_rows, qp_rows, jnp.arange(rows)))
    return o.transpose(1, 0, 2, 3, 4).reshape(B, N, H * Dh)


def hgrn2_forget(z, lb):
    z = z.astype(jnp.float32)
    f = lb + (1.0 - lb) * jax.nn.sigmoid(z)
    log_f = jnp.log(jnp.maximum(f, F_MIN))
    k = (1.0 - lb) * jax.nn.sigmoid(-z)
    return log_f, k


def hgrn2_chunk_scan(q, k, v, log_f, s0):
    B, T, H, DK = q.shape
    DV = v.shape[-1]
    nc = T // HG_CHUNK

    def to_chunks(a):
        return a.reshape(B, nc, HG_CHUNK, H, a.shape[-1]).transpose(1, 0, 3, 2, 4)

    tri = jnp.tril(jnp.ones((HG_CHUNK, HG_CHUNK), dtype=bool))[None, None, :, :, None]

    def step(S, inp):
        qc, kc, vc, gc = inp
        b = jnp.cumsum(gc, axis=2)
        diff = jnp.where(tri, b[:, :, :, None, :] - b[:, :, None, :, :], 0.0)
        decay = jnp.where(tri, jnp.exp(diff), 0.0)
        att = jnp.einsum('bhtc,bhtsc,bhsc->bhts', qc, decay, kc)
        o = (jnp.einsum('bhts,bhsv->bhtv', att, vc)
             + jnp.einsum('bhtc,bhcv->bhtv', qc * jnp.exp(b), S))
        b_end = b[:, :, -1:]
        S = (jnp.exp(b_end[:, :, 0])[..., None] * S
             + jnp.einsum('bhsc,bhsv->bhcv', kc * jnp.exp(b_end - b), vc))
        return S, o

    S, o = lax.scan(step, s0, (to_chunks(q), to_chunks(k), to_chunks(v), to_chunks(log_f)))
    return o.transpose(1, 0, 3, 2, 4).reshape(B, T, H, DV), S


def hgrn2_bidir(p, lb, s0_f, s0_b):
    B, T, _ = p.shape
    heads = lambda j: split_heads(p, j, W_HG).astype(jnp.float32)
    q = jax.nn.silu(heads(0)) * HEAD_DIM ** -0.5
    v = heads(1)
    lf_f, k_f = hgrn2_forget(heads(2), lb[0].reshape(N_HG_HEADS, HEAD_DIM))
    lf_b, k_b = hgrn2_forget(heads(3), lb[1].reshape(N_HG_HEADS, HEAD_DIM))
    o_f, s_f = hgrn2_chunk_scan(q, k_f, v, lf_f, s0_f)
    flip = lambda a: jnp.flip(a, axis=1)
    o_b, s_b = hgrn2_chunk_scan(flip(q), flip(k_b), flip(v), flip(lf_b), s0_b)
    return o_f + flip(o_b), s_f, s_b


def hgrn2_output(o, g, gain):
    B, T = o.shape[:2]
    on = rms_norm(o, gain.reshape(N_HG_HEADS, HEAD_DIM)).reshape(B, T, W_HG)
    return (on * jax.nn.silu(g.astype(jnp.float32))).astype(g.dtype)


def expert_choice_ffn(h, router, w_gate, w_up, w_down):
    B, n, _ = h.shape
    cap = EC_CAPACITY * n // N_EXPERTS
    aff = jax.nn.softmax((h @ router).astype(jnp.float32), axis=-1)
    gate, idx = lax.top_k(aff.transpose(0, 2, 1), cap)
    bi = jnp.arange(B)[:, None, None]
    xe = h[bi, idx]
    u = jnp.einsum('becd,edf->becf', xe, w_gate)
    w = jnp.einsum('becd,edf->becf', xe, w_up)
    y = jnp.einsum('becf,efd->becd', jax.nn.silu(u) * w, w_down) * gate[..., None].astype(h.dtype)
    return jnp.zeros_like(h).at[bi, idx].add(y)


def setup_inputs(seed: int = 0) -> dict:
    key = jax.random.key(seed)
    ks = jax.random.split(key, 24)

    def nrm(k, shape, scale):
        return jax.random.normal(k, shape, jnp.float32) * scale

    return {
        'x': nrm(ks[0], (BATCH, SEQ, D_MODEL), 1.0),
        'c': nrm(ks[1], (BATCH, D_MODEL), 1.0),
        'ctx': nrm(ks[2], (BATCH, CTX_LEN, D_MODEL), 1.0),
        'c_ctx': nrm(ks[3], (D_MODEL,), 1.0),
        'mod_down': nrm(ks[4], (DEPTH, D_MODEL, MOD_RANK), D_MODEL ** -0.5),
        'mod_up': nrm(ks[5], (DEPTH, MOD_RANK, N_MOD * D_MODEL), 0.5 * MOD_RANK ** -0.5),
        'mod_bias': nrm(ks[6], (DEPTH, N_MOD * D_MODEL), 0.02),
        'mix_norm': 1.0 + nrm(ks[7], (DEPTH, D_MODEL), 0.02),
        'ffn_norm': 1.0 + nrm(ks[8], (DEPTH, D_MODEL), 0.02),
        'w_in': nrm(ks[9], (DEPTH, D_MODEL, IN_WIDTH), D_MODEL ** -0.5),
        'w_out': nrm(ks[10], (DEPTH, MIX_WIDTH, D_MODEL), MIX_WIDTH ** -0.5),
        'conv_w': nrm(ks[11], (DEPTH, CONV_K, W_CONV), CONV_K ** -0.5),
        'conv_b': nrm(ks[12], (DEPTH, W_CONV), 0.02),
        'conv_ln_g': 1.0 + nrm(ks[13], (DEPTH, W_CONV), 0.02),
        'conv_ln_b': nrm(ks[14], (DEPTH, W_CONV), 0.02),
        'na_rpb': nrm(ks[15], (DEPTH, N_NA_HEADS, 2 * NA_ROWS - 1, 2 * NA_COLS - 1), 0.1),
        'hg_lb_logits': nrm(ks[16], (2, DEPTH, W_HG), 0.5),
        'hg_norm': 1.0 + nrm(ks[17], (DEPTH, W_HG), 0.02),
        'router': nrm(ks[18], (DEPTH, D_MODEL, N_EXPERTS), D_MODEL ** -0.5),
        'w_gate': nrm(ks[19], (DEPTH, N_EXPERTS, D_MODEL, D_EXPERT), D_MODEL ** -0.5),
        'w_up': nrm(ks[20], (DEPTH, N_EXPERTS, D_MODEL, D_EXPERT), D_MODEL ** -0.5),
        'w_down': nrm(ks[21], (DEPTH, N_EXPERTS, D_EXPERT, D_MODEL), D_EXPERT ** -0.5),
        'final_norm': 1.0 + nrm(ks[22], (D_MODEL,), 0.02),
    }


def reference(x, c, ctx, c_ctx, mod_down, mod_up, mod_bias, mix_norm, ffn_norm,
              w_in, w_out, conv_w, conv_b, conv_ln_g, conv_ln_b, na_rpb,
              hg_lb_logits, hg_norm, router, w_gate, w_up, w_down, final_norm):
    B, N, _ = x.shape
    rows = N // GRID_W
    lb_p = jax.nn.softmax(hg_lb_logits.astype(jnp.float32), axis=1)
    lb_all = jnp.cumsum(lb_p, axis=1) - lb_p[:, :1]
    s_zero = jnp.zeros((B, N_HG_HEADS, HEAD_DIM, HEAD_DIM), jnp.float32)
    a_lat = jax.nn.silu(c)
    a_ctx = jax.nn.silu(c_ctx)
    xc = ctx
    for l in range(DEPTH):
        last = l == DEPTH - 1
        mod = (a_lat @ mod_down[l]) @ mod_up[l] + mod_bias[l]
        mod_c = (a_ctx @ mod_down[l]) @ mod_up[l] + mod_bias[l]
        sh1, sc1, g1, sh2, sc2, g2 = jnp.split(mod[:, None, :], N_MOD, axis=-1)
        csh1, csc1, cg1, csh2, csc2, cg2 = jnp.split(mod_c, N_MOD, axis=-1)

        h = rms_norm(x, mix_norm[l]) * (1 + sc1) + sh1
        hc = rms_norm(xc, mix_norm[l]) * (1 + csc1) + csh1
        p = h @ w_in[l]
        pc = hc @ w_in[l]
        p_na, pc_na = p[..., IN_CONV:IN_CONV + IN_NA], pc[..., IN_CONV:IN_CONV + IN_NA]
        p_hg, pc_hg = p[..., IN_CONV + IN_NA:], pc[..., IN_CONV + IN_NA:]

        ya = conformer_conv(p[..., :IN_CONV], conv_w[l], conv_b[l], conv_ln_g[l], conv_ln_b[l])
        q, k, v = split_heads(p_na, 0, W_NA), split_heads(p_na, 1, W_NA), split_heads(p_na, 2, W_NA)
        qc, kc, vc = split_heads(pc_na, 0, W_NA), split_heads(pc_na, 1, W_NA), split_heads(pc_na, 2, W_NA)
        yb = neighbourhood_attention(q, k, v, kc, vc, na_rpb[l], rows)
        oc, s_f, s_b = hgrn2_bidir(pc_hg, lb_all[:, l], s_zero, s_zero)
        o, _, _ = hgrn2_bidir(p_hg, lb_all[:, l], s_f, s_b)
        yc = hgrn2_output(o, p_hg[..., 4 * W_HG:], hg_norm[l])

        y = jnp.concatenate([ya, yb, yc.astype(ya.dtype)], axis=-1).astype(x.dtype) @ w_out[l]
        x = x + g1 * y

        h2 = rms_norm(x, ffn_norm[l]) * (1 + sc2) + sh2
        x = x + g2 * expert_choice_ffn(h2, router[l], w_gate[l], w_up[l], w_down[l])

        if not last:
            yac = conformer_conv(pc[..., :IN_CONV], conv_w[l], conv_b[l], conv_ln_g[l], conv_ln_b[l])
            ybc = context_attention(qc, kc, vc)
            ycc = hgrn2_output(oc, pc_hg[..., 4 * W_HG:], hg_norm[l])
            yctx = jnp.concatenate([yac, ybc, ycc.astype(yac.dtype)], axis=-1).astype(xc.dtype) @ w_out[l]
            xc = xc + cg1 * yctx
            hc2 = rms_norm(xc, ffn_norm[l]) * (1 + csc2) + csh2
            xc = xc + cg2 * expert_choice_ffn(hc2, router[l], w_gate[l], w_up[l], w_down[l])
    return rms_norm(x, final_norm)
```

```python
import functools

import numpy as np
import jax
import jax.numpy as jnp
from jax import lax
from jax.experimental import pallas as pl
from jax.experimental.pallas import tpu as pltpu

GRID_W = 64
HEAD_DIM = 128
CONV_K = 31
NA_ROWS = 8
NA_COLS = 16
ROPE_THETA = 10000.0
F_MIN = 1e-6
N_EXPERTS = 16
EC_CAPACITY = 2
N_MOD = 6
NORM_EPS = 1e-6

BF16 = jnp.bfloat16
F32 = jnp.float32

VMEM_LIMIT_BYTES = 56 * 1024 * 1024
NEG_BIG = -1e30
NA_TILE_ROWS = 8
NA_TQ = NA_TILE_ROWS * GRID_W
HG_SUB = 16
HG_TB = 512


def _cparams(*sem):
    return pltpu.CompilerParams(dimension_semantics=sem, vmem_limit_bytes=VMEM_LIMIT_BYTES)


def _tile(n, preferred):
    t = preferred
    while n % t:
        t //= 2
    return t


def _norm_mod(x, g, sc, sh):
    r = lax.rsqrt(jnp.mean(x * x, axis=-1, keepdims=True) + NORM_EPS)
    return (x * r * g) * (1.0 + sc) + sh


def _nmm_body(x_ref, g_ref, sc_ref, sh_ref, w_ref, o_ref, h_scr):
    @pl.when(pl.program_id(1) == 0)
    def _():
        h_scr[...] = _norm_mod(x_ref[...], g_ref[...], sc_ref[0], sh_ref[0]).astype(BF16)

    o_ref[...] = jnp.dot(h_scr[...], w_ref[...], preferred_element_type=F32)


def norm_mod_matmul(x, g, sc, sh, w, *, tm, tn):
    M, D = x.shape
    N = w.shape[1]
    G = sc.shape[0]
    tiles_per_group = M // G // tm
    return pl.pallas_call(
        _nmm_body,
        grid=(M // tm, N // tn),
        in_specs=[
            pl.BlockSpec((tm, D), lambda i, j: (i, 0)),
            pl.BlockSpec((1, D), lambda i, j: (0, 0)),
            pl.BlockSpec((1, 1, D), lambda i, j: (i // tiles_per_group, 0, 0)),
            pl.BlockSpec((1, 1, D), lambda i, j: (i // tiles_per_group, 0, 0)),
            pl.BlockSpec((D, tn), lambda i, j: (0, j)),
        ],
        out_specs=pl.BlockSpec((tm, tn), lambda i, j: (i, j)),
        out_shape=jax.ShapeDtypeStruct((M, N), F32),
        scratch_shapes=[pltpu.VMEM((tm, D), BF16)],
        compiler_params=_cparams("parallel", "arbitrary"),
        name="norm_mod_matmul",
    )(x, g, sc, sh, w)


def _nm_body(x_ref, g_ref, sc_ref, sh_ref, o_ref):
    o_ref[...] = _norm_mod(x_ref[...], g_ref[...], sc_ref[0], sh_ref[0])


def norm_mod(x, g, sc, sh, *, tm):
    M, D = x.shape
    G = sc.shape[0]
    tiles_per_group = M // G // tm
    return pl.pallas_call(
        _nm_body,
        grid=(M // tm,),
        in_specs=[
            pl.BlockSpec((tm, D), lambda i: (i, 0)),
            pl.BlockSpec((1, D), lambda i: (0, 0)),
            pl.BlockSpec((1, 1, D), lambda i: (i // tiles_per_group, 0, 0)),
            pl.BlockSpec((1, 1, D), lambda i: (i // tiles_per_group, 0, 0)),
        ],
        out_specs=pl.BlockSpec((tm, D), lambda i: (i, 0)),
        out_shape=jax.ShapeDtypeStruct((M, D), F32),
        compiler_params=_cparams("parallel"),
        name="norm_mod",
    )(x, g, sc, sh)


def _outproj_body(ya_ref, yb_ref, yc_ref, wa_ref, wb_ref, wc_ref, x_ref, g_ref, o_ref):
    acc = jnp.dot(ya_ref[...], wa_ref[...], preferred_element_type=F32)
    acc += jnp.dot(yb_ref[...], wb_ref[...], preferred_element_type=F32)
    acc += jnp.dot(yc_ref[...], wc_ref[...], preferred_element_type=F32)
    o_ref[...] = x_ref[...] + g_ref[0] * acc


def outproj_residual(ya, yb, yc, wa, wb, wc, x, gate, *, tm, tn):
    M, D = x.shape
    G = gate.shape[0]
    tiles_per_group = M // G // tm
    ka, kb, kc = ya.shape[1], yb.shape[1], yc.shape[1]
    return pl.pallas_call(
        _outproj_body,
        grid=(M // tm, D // tn),
        in_specs=[
            pl.BlockSpec((tm, ka), lambda i, j: (i, 0)),
            pl.BlockSpec((tm, kb), lambda i, j: (i, 0)),
            pl.BlockSpec((tm, kc), lambda i, j: (i, 0)),
            pl.BlockSpec((ka, tn), lambda i, j: (0, j)),
            pl.BlockSpec((kb, tn), lambda i, j: (0, j)),
            pl.BlockSpec((kc, tn), lambda i, j: (0, j)),
            pl.BlockSpec((tm, tn), lambda i, j: (i, j)),
            pl.BlockSpec((1, 1, tn), lambda i, j: (i // tiles_per_group, 0, j)),
        ],
        out_specs=pl.BlockSpec((tm, tn), lambda i, j: (i, j)),
        out_shape=jax.ShapeDtypeStruct((M, D), F32),
        compiler_params=_cparams("parallel", "arbitrary"),
        name="outproj_residual",
    )(ya, yb, yc, wa, wb, wc, x, gate)


def _rope_tables(n_tokens):
    half = HEAD_DIM // 2
    n_freq = half // 2
    t = np.arange(n_tokens)
    pos = np.stack([t // GRID_W, t % GRID_W], axis=1).astype(np.float32)
    d = np.arange(HEAD_DIM)
    which = d // half
    sign = np.where((d % half) < n_freq, -1.0, 1.0).astype(np.float32)
    inv_freq = ROPE_THETA ** (-jnp.arange(n_freq, dtype=F32) / n_freq)
    ang = jnp.asarray(pos)[:, which] * inv_freq[d % n_freq][None, :]
    return jnp.cos(ang), jnp.sin(ang) * sign[None, :]


def _na_geometry(rows):
    n_tiles = rows // NA_TILE_ROWS
    kr = min(NA_ROWS, rows)
    i = np.arange(NA_TQ)
    j = np.arange(3 * NA_TQ)
    valid, ridx, cidx = [], [], []
    for t in (0, min(1, n_tiles - 1), n_tiles - 1):
        qrow = (NA_TILE_ROWS * t + i // GRID_W)[:, None]
        qcol = (i % GRID_W)[:, None]
        krow = (NA_TILE_ROWS * (t - 1) + j // GRID_W)[None, :]
        kcol = (j % GRID_W)[None, :]
        rs = np.clip(qrow - kr // 2, 0, rows - kr)
        cs = np.clip(qcol - NA_COLS // 2, 0, GRID_W - NA_COLS)
        ok = ((krow >= rs) & (krow < rs + kr) & (krow >= 0) & (krow < rows)
              & (kcol >= cs) & (kcol < cs + NA_COLS))
        if t == 0:
            ok &= (j >= NA_TQ)[None, :]
        if t == n_tiles - 1:
            ok &= (j < 2 * NA_TQ)[None, :]
        valid.append(ok)
        ridx.append(np.where(ok, krow - qrow + NA_ROWS - 1, 0))
        cidx.append(np.where(ok, kcol - qcol + NA_COLS - 1, 0))
    return np.stack(valid), np.stack(ridx), np.stack(cidx)


def _na_bias(rpb, rows):
    valid, ridx, cidx = _na_geometry(rows)
    flat = rpb.reshape(rpb.shape[0], -1)
    idx = jnp.asarray(ridx * (2 * NA_COLS - 1) + cidx, jnp.int32)
    bias = jnp.take(flat, idx, axis=1)
    bias = jnp.where(jnp.asarray(valid)[None], bias, NEG_BIG)
    return bias.transpose(1, 0, 2, 3)


def _rope(x, cos, sin, first):
    partner = jnp.where(first, pltpu.roll(x, 3 * HEAD_DIM // 4, 1), pltpu.roll(x, HEAD_DIM // 4, 1))
    return x * cos + partner * sin


def _na_body(q_ref, kp_ref, kc_ref, kn_ref, vp_ref, vc_ref, vn_ref, kx_ref, vx_ref,
             cp_ref, cc_ref, cn_ref, sp_ref, sc_ref, sn_ref, bias_ref, o_ref):
    scale = HEAD_DIM ** -0.5
    nt = (((1,), (1,)), ((), ()))
    lane = lax.broadcasted_iota(jnp.int32, (NA_TQ, HEAD_DIM), 1)
    first = (lane % (HEAD_DIM // 2)) < (HEAD_DIM // 4)
    q = q_ref[...]
    qr = _rope(q, cc_ref[...], sc_ref[...], first).astype(BF16)
    kr = jnp.concatenate([
        _rope(kp_ref[...], cp_ref[...], sp_ref[...], first).astype(BF16),
        _rope(kc_ref[...], cc_ref[...], sc_ref[...], first).astype(BF16),
        _rope(kn_ref[...], cn_ref[...], sn_ref[...], first).astype(BF16)], axis=0)
    s_lat = lax.dot_general(qr, kr, nt, preferred_element_type=F32) * scale + bias_ref[0, 0]
    s_ctx = lax.dot_general(q.astype(BF16), kx_ref[...].astype(BF16), nt,
                            preferred_element_type=F32) * scale
    m = jnp.maximum(jnp.max(s_lat, axis=-1, keepdims=True), jnp.max(s_ctx, axis=-1, keepdims=True))
    p_lat = jnp.exp(s_lat - m)
    p_ctx = jnp.exp(s_ctx - m)
    denom = jnp.sum(p_lat, axis=-1, keepdims=True) + jnp.sum(p_ctx, axis=-1, keepdims=True)
    v_all = jnp.concatenate([vp_ref[...].astype(BF16), vc_ref[...].astype(BF16),
                             vn_ref[...].astype(BF16)], axis=0)
    o = (jnp.dot(p_lat.astype(BF16), v_all, preferred_element_type=F32)
         + jnp.dot(p_ctx.astype(BF16), vx_ref[...].astype(BF16), preferred_element_type=F32))
    o_ref[...] = (o / denom).astype(o_ref.dtype)


def neighbourhood_attention(p, pc, bias, cos_tab, sin_tab, *, batch, n_heads, col_q, ctx_len):
    n_tok = p.shape[0] // batch
    n_tiles = n_tok // NA_TQ
    col_k, col_v = col_q + n_heads, col_q + 2 * n_heads
    last = n_tiles - 1

    def tile(h, b, t):
        return b * n_tiles + t

    def prev(h, b, t):
        return b * n_tiles + jnp.maximum(t - 1, 0)

    def nxt(h, b, t):
        return b * n_tiles + jnp.minimum(t + 1, last)

    def variant(h, b, t):
        return jnp.where(t == 0, 0, jnp.where(t == last, 2, 1))

    blk = (NA_TQ, HEAD_DIM)
    ctx_blk = (ctx_len, HEAD_DIM)
    in_specs = [
        pl.BlockSpec(blk, lambda h, b, t: (tile(h, b, t), col_q + h)),
        pl.BlockSpec(blk, lambda h, b, t: (prev(h, b, t), col_k + h)),
        pl.BlockSpec(blk, lambda h, b, t: (tile(h, b, t), col_k + h)),
        pl.BlockSpec(blk, lambda h, b, t: (nxt(h, b, t), col_k + h)),
        pl.BlockSpec(blk, lambda h, b, t: (prev(h, b, t), col_v + h)),
        pl.BlockSpec(blk, lambda h, b, t: (tile(h, b, t), col_v + h)),
        pl.BlockSpec(blk, lambda h, b, t: (nxt(h, b, t), col_v + h)),
        pl.BlockSpec(ctx_blk, lambda h, b, t: (b, col_k + h)),
        pl.BlockSpec(ctx_blk, lambda h, b, t: (b, col_v + h)),
        pl.BlockSpec(blk, lambda h, b, t: (jnp.maximum(t - 1, 0), 0)),
        pl.BlockSpec(blk, lambda h, b, t: (t, 0)),
        pl.BlockSpec(blk, lambda h, b, t: (jnp.minimum(t + 1, last), 0)),
        pl.BlockSpec(blk, lambda h, b, t: (jnp.maximum(t - 1, 0), 0)),
        pl.BlockSpec(blk, lambda h, b, t: (t, 0)),
        pl.BlockSpec(blk, lambda h, b, t: (jnp.minimum(t + 1, last), 0)),
        pl.BlockSpec((1, 1, NA_TQ, 3 * NA_TQ), lambda h, b, t: (variant(h, b, t), h, 0, 0)),
    ]
    return pl.pallas_call(
        _na_body,
        grid=(n_heads, batch, n_tiles),
        in_specs=in_specs,
        out_specs=pl.BlockSpec(blk, lambda h, b, t: (tile(h, b, t), h)),
        out_shape=jax.ShapeDtypeStruct((p.shape[0], n_heads * HEAD_DIM), BF16),
        compiler_params=_cparams("parallel", "parallel", "arbitrary"),
        name="neighbourhood_attention",
    )(p, p, p, p, p, p, p, pc, pc, cos_tab, cos_tab, cos_tab, sin_tab, sin_tab, sin_tab, bias)


def _ctx_attn_body(q_ref, k_ref, v_ref, o_ref):
    nt = (((1,), (1,)), ((), ()))
    s = lax.dot_general(q_ref[...].astype(BF16), k_ref[...].astype(BF16), nt,
                        preferred_element_type=F32) * HEAD_DIM ** -0.5
    p = jnp.exp(s - jnp.max(s, axis=-1, keepdims=True))
    denom = jnp.sum(p, axis=-1, keepdims=True)
    o = jnp.dot(p.astype(BF16), v_ref[...].astype(BF16), preferred_element_type=F32)
    o_ref[...] = (o / denom).astype(o_ref.dtype)


def context_attention(pc, *, batch, n_heads, col_q, ctx_len):
    col_k, col_v = col_q + n_heads, col_q + 2 * n_heads
    blk = (ctx_len, HEAD_DIM)
    return pl.pallas_call(
        _ctx_attn_body,
        grid=(batch, n_heads),
        in_specs=[pl.BlockSpec(blk, lambda b, h: (b, col_q + h)),
                  pl.BlockSpec(blk, lambda b, h: (b, col_k + h)),
                  pl.BlockSpec(blk, lambda b, h: (b, col_v + h))],
        out_specs=pl.BlockSpec(blk, lambda b, h: (b, h)),
        out_shape=jax.ShapeDtypeStruct((pc.shape[0], n_heads * HEAD_DIM), BF16),
        compiler_params=_cparams("parallel", "parallel"),
        name="context_attention",
    )(pc, pc, pc)


def _hg_prep(zq, zf, lb, tri):
    q = zq * jax.nn.sigmoid(zq) * HEAD_DIM ** -0.5
    f = lb + (1.0 - lb) * jax.nn.sigmoid(zf)
    log_f = jnp.log(jnp.maximum(f, F_MIN))
    k = (1.0 - lb) * jax.nn.sigmoid(-zf)
    b = jnp.dot(tri, log_f, preferred_element_type=F32, precision=lax.Precision.HIGHEST)
    return q, k, b


def _hg_sub_block(r, q_s, k_s, b_s, v_ref, o_ref, st_s, d, reverse):
    nt = (((1,), (1,)), ((), ()))
    tn = (((0,), (0,)), ((), ()))
    bI = b_s[pl.ds(r, HG_SUB), :]
    qI = q_s[pl.ds(r, HG_SUB), :]
    kI = k_s[pl.ds(r, HG_SUB), :]
    vI = v_ref[pl.ds(r, HG_SUB), :]
    st = st_s[d]
    o_inter = lax.dot_general((qI * jnp.exp(bI)).astype(BF16), st.astype(BF16), nt,
                              preferred_element_type=F32)
    sub = lax.broadcasted_iota(jnp.int32, (HG_SUB, HEAD_DIM), 0)
    slabs = []
    for t in range(HG_SUB):
        keep = (sub >= t) if reverse else (sub <= t)
        decay = jnp.where(keep, jnp.exp(bI[t:t + 1, :] - bI), 0.0)
        slabs.append(decay * (qI[t:t + 1, :] * kI))
    x3 = jnp.concatenate(slabs, axis=0).astype(BF16)
    att = jnp.dot(x3, jnp.ones((HEAD_DIM, HEAD_DIM), BF16), preferred_element_type=F32)
    z = att * jnp.concatenate([vI] * HG_SUB, axis=0)
    o_intra = jnp.sum(z.reshape(HG_SUB, HG_SUB, HEAD_DIM), axis=1)
    o_ref[pl.ds(r, HG_SUB), :] = o_inter + o_intra
    b_end = bI[0:1, :] if reverse else bI[HG_SUB - 1:HG_SUB, :]
    k_dec = (kI * jnp.exp(b_end - bI)).astype(BF16)
    upd = lax.dot_general(vI.astype(BF16), k_dec, tn, preferred_element_type=F32)
    st_s[d] = st * jnp.exp(b_end) + upd


def _hg_body(zqf_ref, zvf_ref, zff_ref, zqb_ref, zvb_ref, zfb_ref, lbf_ref, lbb_ref,
             tril_ref, triu_ref, s0_ref, of_ref, ob_ref, sout_ref,
             qf_s, kf_s, bf_s, qb_s, kb_s, bb_s, st_s, *, tb):
    c = pl.program_id(1)

    @pl.when(c == 0)
    def _():
        st_s[...] = s0_ref[...]

    qf, kf, bf = _hg_prep(zqf_ref[...], zff_ref[...], lbf_ref[...], tril_ref[...])
    qf_s[...] = qf
    kf_s[...] = kf
    bf_s[...] = bf
    qb, kb, bb = _hg_prep(zqb_ref[...], zfb_ref[...], lbb_ref[...], triu_ref[...])
    qb_s[...] = qb
    kb_s[...] = kb
    bb_s[...] = bb
    n_sub = tb // HG_SUB

    def step(i, carry):
        rf = pl.multiple_of(i * HG_SUB, HG_SUB)
        _hg_sub_block(rf, qf_s, kf_s, bf_s, zvf_ref, of_ref, st_s, 0, False)
        rb = pl.multiple_of((n_sub - 1 - i) * HG_SUB, HG_SUB)
        _hg_sub_block(rb, qb_s, kb_s, bb_s, zvb_ref, ob_ref, st_s, 1, True)
        return carry

    lax.fori_loop(0, n_sub, step, 0)

    @pl.when(c == pl.num_programs(1) - 1)
    def _():
        sout_ref[...] = st_s[...]


def hgrn2_bidir(p, lb_f, lb_b, s0, *, batch, n_heads, col0, tb):
    n_tok = p.shape[0] // batch
    nb = n_tok // tb
    H = n_heads
    eye = np.kron(np.eye(tb // HG_SUB), np.ones((HG_SUB, HG_SUB)))
    tril = jnp.asarray(np.tril(eye), F32)
    triu = jnp.asarray(np.triu(eye), F32)

    def fwd(col):
        return pl.BlockSpec((tb, HEAD_DIM), lambda bh, c: ((bh // H) * nb + c, col + bh % H))

    def bwd(col):
        return pl.BlockSpec((tb, HEAD_DIM), lambda bh, c: ((bh // H) * nb + nb - 1 - c, col + bh % H))

    def out(rev):
        if rev:
            return pl.BlockSpec((tb, HEAD_DIM), lambda bh, c: ((bh // H) * nb + nb - 1 - c, bh % H))
        return pl.BlockSpec((tb, HEAD_DIM), lambda bh, c: ((bh // H) * nb + c, bh % H))

    lb_spec = pl.BlockSpec((None, 1, HEAD_DIM), lambda bh, c: (bh % H, 0, 0))
    tri_spec = pl.BlockSpec((tb, tb), lambda bh, c: (0, 0))
    st_spec = pl.BlockSpec((None, 2, HEAD_DIM, HEAD_DIM), lambda bh, c: (bh, 0, 0, 0))
    o_shape = jax.ShapeDtypeStruct((p.shape[0], H * HEAD_DIM), F32)
    return pl.pallas_call(
        functools.partial(_hg_body, tb=tb),
        grid=(batch * H, nb),
        in_specs=[fwd(col0), fwd(col0 + H), fwd(col0 + 2 * H),
                  bwd(col0), bwd(col0 + H), bwd(col0 + 3 * H),
                  lb_spec, lb_spec, tri_spec, tri_spec, st_spec],
        out_specs=[out(False), out(True), st_spec],
        out_shape=[o_shape, o_shape, jax.ShapeDtypeStruct(s0.shape, F32)],
        scratch_shapes=[pltpu.VMEM((tb, HEAD_DIM), F32)] * 6 + [pltpu.VMEM((2, HEAD_DIM, HEAD_DIM), F32)],
        compiler_params=_cparams("parallel", "arbitrary"),
        name="hgrn2_bidir",
    )(p, p, p, p, p, p, lb_f, lb_b, tril, triu, s0)


def _hg_out_body(of_ref, ob_ref, g_ref, gain_ref, y_ref):
    o = of_ref[...] + ob_ref[...]
    on = o * lax.rsqrt(jnp.mean(o * o, axis=-1, keepdims=True) + NORM_EPS) * gain_ref[...]
    g = g_ref[...]
    y_ref[...] = (on * (g * jax.nn.sigmoid(g))).astype(y_ref.dtype)


def hgrn2_output(o_f, o_b, p, gain, *, n_heads, col_g, tm):
    M = o_f.shape[0]
    blk = (tm, HEAD_DIM)
    return pl.pallas_call(
        _hg_out_body,
        grid=(M // tm, n_heads),
        in_specs=[pl.BlockSpec(blk, lambda i, h: (i, h)),
                  pl.BlockSpec(blk, lambda i, h: (i, h)),
                  pl.BlockSpec(blk, lambda i, h: (i, col_g + h)),
                  pl.BlockSpec((None, 1, HEAD_DIM), lambda i, h: (h, 0, 0))],
        out_specs=pl.BlockSpec(blk, lambda i, h: (i, h)),
        out_shape=jax.ShapeDtypeStruct(o_f.shape, BF16),
        compiler_params=_cparams("parallel", "parallel"),
        name="hgrn2_output",
    )(o_f, o_b, p, gain)


def _ffn_body(x_ref, wg_ref, wu_ref, wd_ref, gate_ref, o_ref):
    x = x_ref[...]
    u = jnp.dot(x, wg_ref[...], preferred_element_type=F32)
    w = jnp.dot(x, wu_ref[...], preferred_element_type=F32)
    a = (u * jax.nn.sigmoid(u) * w).astype(BF16)
    o_ref[...] = jnp.dot(a, wd_ref[...], preferred_element_type=F32) * gate_ref[...]


def expert_ffn(xe, w_gate, w_up, w_down, gate, *, tm):
    B, E, cap, D = xe.shape
    F = w_gate.shape[-1]
    return pl.pallas_call(
        _ffn_body,
        grid=(E, B, cap // tm),
        in_specs=[pl.BlockSpec((None, None, tm, D), lambda e, b, r: (b, e, r, 0)),
                  pl.BlockSpec((None, D, F), lambda e, b, r: (e, 0, 0)),
                  pl.BlockSpec((None, D, F), lambda e, b, r: (e, 0, 0)),
                  pl.BlockSpec((None, F, D), lambda e, b, r: (e, 0, 0)),
                  pl.BlockSpec((None, None, tm, 1), lambda e, b, r: (b, e, r, 0))],
        out_specs=pl.BlockSpec((None, None, tm, D), lambda e, b, r: (b, e, r, 0)),
        out_shape=jax.ShapeDtypeStruct((B, E, cap, D), F32),
        compiler_params=_cparams("parallel", "parallel", "arbitrary"),
        name="expert_ffn",
    )(xe, w_gate, w_up, w_down, gate)


def _conformer_conv(a, w_dw, b_dw, ln_g, ln_b):
    w_conv = a.shape[-1] // 2
    u = a[..., :w_conv] * jax.nn.sigmoid(a[..., w_conv:])
    u = lax.conv_general_dilated(
        u, w_dw[:, None, :], window_strides=(1,), padding=[(CONV_K // 2, CONV_K // 2)],
        dimension_numbers=('NWC', 'WIO', 'NWC'), feature_group_count=w_conv) + b_dw
    mu = jnp.mean(u, axis=-1, keepdims=True)
    var = jnp.mean(jnp.square(u - mu), axis=-1, keepdims=True)
    un = (u - mu) * lax.rsqrt(var + NORM_EPS) * ln_g + ln_b
    return jax.nn.silu(un).astype(BF16)


def _expert_choice_ffn(h, router, w_gate, w_up, w_down):
    B, n, D = h.shape
    cap = EC_CAPACITY * n // N_EXPERTS
    logits = jnp.dot(h, router, precision=lax.Precision.HIGHEST)
    aff = jax.nn.softmax(logits, axis=-1)
    gate, idx = lax.top_k(aff.transpose(0, 2, 1), cap)
    bi = jnp.arange(B)[:, None, None]
    xe = h.astype(BF16)[bi, idx]
    y = expert_ffn(xe, w_gate, w_up, w_down, gate[..., None], tm=min(cap, 512))
    return jnp.zeros_like(h).at[bi, idx].add(y)


def kernel(x, c, ctx, c_ctx, mod_down, mod_up, mod_bias, mix_norm, ffn_norm, w_in, w_out,
           conv_w, conv_b, conv_ln_g, conv_ln_b, na_rpb, hg_lb_logits, hg_norm, router,
           w_gate, w_up, w_down, final_norm):
    B, N, D = x.shape
    M = ctx.shape[1]
    depth = w_in.shape[0]
    rows = N // GRID_W
    w_conv = conv_w.shape[-1]
    w_hg = hg_norm.shape[-1]
    n_hg = w_hg // HEAD_DIM
    n_na = na_rpb.shape[1]
    w_na = n_na * HEAD_DIM
    in_conv = 2 * w_conv
    col_na = in_conv // HEAD_DIM
    col_hg = (in_conv + 3 * w_na) // HEAD_DIM

    lb_p = jax.nn.softmax(hg_lb_logits.astype(F32), axis=1)
    lb_all = jnp.cumsum(lb_p, axis=1) - lb_p[:, :1]
    a_lat = jax.nn.silu(c)
    a_ctx = jax.nn.silu(c_ctx)
    cos_tab, sin_tab = _rope_tables(N)
    s_zero = jnp.zeros((B * n_hg, 2, HEAD_DIM, HEAD_DIM), F32)

    xl = x.reshape(B * N, D)
    xc = ctx.reshape(B * M, D)
    for l in range(depth):
        last = l == depth - 1
        hi = lax.Precision.HIGHEST
        mod = jnp.dot(jnp.dot(a_lat, mod_down[l], precision=hi), mod_up[l], precision=hi) + mod_bias[l]
        mod_c = jnp.dot(jnp.dot(a_ctx, mod_down[l], precision=hi), mod_up[l], precision=hi) + mod_bias[l]
        sh1, sc1, g1, sh2, sc2, g2 = [m[:, None, :] for m in jnp.split(mod, N_MOD, axis=-1)]
        csh1, csc1, cg1, csh2, csc2, cg2 = [m[None, None, :] for m in jnp.split(mod_c, N_MOD, axis=-1)]

        w_in_l = w_in[l].astype(BF16)
        wo = w_out[l]
        wo_a = wo[:w_conv].astype(BF16)
        wo_b = wo[w_conv:w_conv + w_na].astype(BF16)
        wo_c = wo[w_conv + w_na:].astype(BF16)
        wg_l, wu_l, wd_l = w_gate[l].astype(BF16), w_up[l].astype(BF16), w_down[l].astype(BF16)
        g_mix = mix_norm[l][None, :]
        g_ffn = ffn_norm[l][None, :]

        tn_in = _tile(w_in_l.shape[1], 1024)
        tn_out = _tile(D, 1024)
        p = norm_mod_matmul(xl, g_mix, sc1, sh1, w_in_l, tm=512, tn=tn_in)
        pc = norm_mod_matmul(xc, g_mix, csc1, csh1, w_in_l, tm=B * M, tn=tn_in)

        ya = _conformer_conv(p[:, :in_conv].reshape(B, N, in_conv), conv_w[l], conv_b[l],
                             conv_ln_g[l], conv_ln_b[l]).reshape(B * N, w_conv)
        bias = _na_bias(na_rpb[l], rows)
        yb = neighbourhood_attention(p, pc, bias, cos_tab, sin_tab, batch=B, n_heads=n_na,
                                     col_q=col_na, ctx_len=M)
        lb_f = lb_all[0, l].reshape(n_hg, 1, HEAD_DIM)
        lb_b = lb_all[1, l].reshape(n_hg, 1, HEAD_DIM)
        oc_f, oc_b, s_ctx = hgrn2_bidir(pc, lb_f, lb_b, s_zero, batch=B, n_heads=n_hg, col0=col_hg, tb=M)
        o_f, o_b, _ = hgrn2_bidir(p, lb_f, lb_b, s_ctx, batch=B, n_heads=n_hg, col0=col_hg, tb=HG_TB)
        gain = hg_norm[l].reshape(n_hg, 1, HEAD_DIM)
        yc = hgrn2_output(o_f, o_b, p, gain, n_heads=n_hg, col_g=col_hg + 4 * n_hg, tm=512)

        xl = outproj_residual(ya, yb, yc, wo_a, wo_b, wo_c, xl, g1, tm=512, tn=tn_out)
        h2 = norm_mod(xl, g_ffn, sc2, sh2, tm=512)
        moe = _expert_choice_ffn(h2.reshape(B, N, D), router[l], wg_l, wu_l, wd_l)
        xl = xl + (g2 * moe).reshape(B * N, D)

        if not last:
            yac = _conformer_conv(pc[:, :in_conv].reshape(B, M, in_conv), conv_w[l], conv_b[l],
                                  conv_ln_g[l], conv_ln_b[l]).reshape(B * M, w_conv)
            ybc = context_attention(pc, batch=B, n_heads=n_na, col_q=col_na, ctx_len=M)
            ycc = hgrn2_output(oc_f, oc_b, pc, gain, n_heads=n_hg, col_g=col_hg + 4 * n_hg, tm=B * M)
            xc = outproj_residual(yac, ybc, ycc, wo_a, wo_b, wo_c, xc, cg1, tm=B * M, tn=tn_out)
            hc2 = norm_mod(xc, g_ffn, csc2, csh2, tm=B * M)
            moe_c = _expert_choice_ffn(hc2.reshape(B, M, D), router[l], wg_l, wu_l, wd_l)
            xc = xc + (cg2 * moe_c).reshape(B * M, D)

    ones = jnp.ones((1, 1, D), F32)
    out = norm_mod(xl, final_norm[None, :], ones * 0.0, ones * 0.0, tm=512)
    return out.reshape(B, N, D)
```

```python
import functools

import numpy as np
import jax
import jax.numpy as jnp
from jax import lax
from jax.experimental import pallas as pl
from jax.experimental.pallas import tpu as pltpu

GRID_W = 64
HEAD_DIM = 128
CONV_K = 31
NA_ROWS = 8
NA_COLS = 16
ROPE_THETA = 10000.0
F_MIN = 1e-6
N_EXPERTS = 16
EC_CAPACITY = 2
N_MOD = 6
NORM_EPS = 1e-6

BF16 = jnp.bfloat16
F32 = jnp.float32

VMEM_LIMIT_BYTES = 56 * 1024 * 1024
NEG_BIG = -1e30
NA_TILE_ROWS = 4
NA_TQ = NA_TILE_ROWS * GRID_W
HG_SUB = 16
HG_CHUNK = 32
HG_SAFE_DECAY = 80.0
HG_TRI = 128
HG_TB = 512


def _cparams(*sem):
    return pltpu.CompilerParams(dimension_semantics=sem, vmem_limit_bytes=VMEM_LIMIT_BYTES)


def _tile(n, preferred):
    t = preferred
    while n % t:
        t //= 2
    return t


def _norm_mod(x, g, sc, sh):
    r = lax.rsqrt(jnp.mean(x * x, axis=-1, keepdims=True) + NORM_EPS)
    return (x * r * g) * (1.0 + sc) + sh


def _nmm_body(x_ref, g_ref, sc_ref, sh_ref, w_ref, o_ref, h_scr):
    @pl.when(pl.program_id(1) == 0)
    def _():
        h_scr[...] = _norm_mod(x_ref[...], g_ref[...], sc_ref[0], sh_ref[0]).astype(BF16)

    o_ref[...] = jnp.dot(h_scr[...], w_ref[...], preferred_element_type=F32)


def norm_mod_matmul(x, g, sc, sh, w, *, tm, tn):
    M, D = x.shape
    N = w.shape[1]
    G = sc.shape[0]
    tiles_per_group = M // G // tm
    return pl.pallas_call(
        _nmm_body,
        grid=(M // tm, N // tn),
        in_specs=[
            pl.BlockSpec((tm, D), lambda i, j: (i, 0)),
            pl.BlockSpec((1, D), lambda i, j: (0, 0)),
            pl.BlockSpec((1, 1, D), lambda i, j: (i // tiles_per_group, 0, 0)),
            pl.BlockSpec((1, 1, D), lambda i, j: (i // tiles_per_group, 0, 0)),
            pl.BlockSpec((D, tn), lambda i, j: (0, j)),
        ],
        out_specs=pl.BlockSpec((tm, tn), lambda i, j: (i, j)),
        out_shape=jax.ShapeDtypeStruct((M, N), F32),
        scratch_shapes=[pltpu.VMEM((tm, D), BF16)],
        compiler_params=_cparams("parallel", "arbitrary"),
        name="norm_mod_matmul",
    )(x, g, sc, sh, w)


def _nm_body(x_ref, g_ref, sc_ref, sh_ref, o_ref):
    o_ref[...] = _norm_mod(x_ref[...], g_ref[...], sc_ref[0], sh_ref[0])


def norm_mod(x, g, sc, sh, *, tm):
    M, D = x.shape
    G = sc.shape[0]
    tiles_per_group = M // G // tm
    return pl.pallas_call(
        _nm_body,
        grid=(M // tm,),
        in_specs=[
            pl.BlockSpec((tm, D), lambda i: (i, 0)),
            pl.BlockSpec((1, D), lambda i: (0, 0)),
            pl.BlockSpec((1, 1, D), lambda i: (i // tiles_per_group, 0, 0)),
            pl.BlockSpec((1, 1, D), lambda i: (i // tiles_per_group, 0, 0)),
        ],
        out_specs=pl.BlockSpec((tm, D), lambda i: (i, 0)),
        out_shape=jax.ShapeDtypeStruct((M, D), F32),
        compiler_params=_cparams("parallel"),
        name="norm_mod",
    )(x, g, sc, sh)


def _outproj_body(ya_ref, yb_ref, yc_ref, wa_ref, wb_ref, wc_ref, x_ref, g_ref, o_ref):
    acc = jnp.dot(ya_ref[...], wa_ref[...], preferred_element_type=F32)
    acc += jnp.dot(yb_ref[...], wb_ref[...], preferred_element_type=F32)
    acc += jnp.dot(yc_ref[...], wc_ref[...], preferred_element_type=F32)
    o_ref[...] = x_ref[...] + g_ref[0] * acc


def outproj_residual(ya, yb, yc, wa, wb, wc, x, gate, *, tm, tn):
    M, D = x.shape
    G = gate.shape[0]
    tiles_per_group = M // G // tm
    ka, kb, kc = ya.shape[1], yb.shape[1], yc.shape[1]
    return pl.pallas_call(
        _outproj_body,
        grid=(M // tm, D // tn),
        in_specs=[
            pl.BlockSpec((tm, ka), lambda i, j: (i, 0)),
            pl.BlockSpec((tm, kb), lambda i, j: (i, 0)),
            pl.BlockSpec((tm, kc), lambda i, j: (i, 0)),
            pl.BlockSpec((ka, tn), lambda i, j: (0, j)),
            pl.BlockSpec((kb, tn), lambda i, j: (0, j)),
            pl.BlockSpec((kc, tn), lambda i, j: (0, j)),
            pl.BlockSpec((tm, tn), lambda i, j: (i, j)),
            pl.BlockSpec((1, 1, tn), lambda i, j: (i // tiles_per_group, 0, j)),
        ],
        out_specs=pl.BlockSpec((tm, tn), lambda i, j: (i, j)),
        out_shape=jax.ShapeDtypeStruct((M, D), F32),
        compiler_params=_cparams("parallel", "arbitrary"),
        name="outproj_residual",
    )(ya, yb, yc, wa, wb, wc, x, gate)


def _rope_tables(n_tokens):
    half = HEAD_DIM // 2
    n_freq = half // 2
    t = np.arange(n_tokens)
    pos = np.stack([t // GRID_W, t % GRID_W], axis=1).astype(np.float32)
    d = np.arange(HEAD_DIM)
    which = d // half
    sign = np.where((d % half) < n_freq, -1.0, 1.0).astype(np.float32)
    inv_freq = ROPE_THETA ** (-jnp.arange(n_freq, dtype=F32) / n_freq)
    ang = jnp.asarray(pos)[:, which] * inv_freq[d % n_freq][None, :]
    return jnp.cos(ang), jnp.sin(ang) * sign[None, :]


def _na_geometry(rows):
    n_tiles = rows // NA_TILE_ROWS
    kr = min(NA_ROWS, rows)
    a = np.arange(NA_TILE_ROWS)[:, None]
    r = np.arange(3 * NA_TILE_ROWS)[None, :]
    row_sel = np.zeros((3, NA_TILE_ROWS, 3 * NA_TILE_ROWS, 2 * NA_ROWS - 1), np.float32)
    for v, t in enumerate((0, min(1, n_tiles - 1), n_tiles - 1)):
        qrow = NA_TILE_ROWS * t + a
        krow = NA_TILE_ROWS * (t - 1) + r
        rs = np.clip(qrow - kr // 2, 0, rows - kr)
        ok = (krow >= rs) & (krow < rs + kr) & (krow >= 0) & (krow < rows)
        aa, rr = np.nonzero(ok)
        row_sel[v, aa, rr, (krow - qrow + NA_ROWS - 1)[aa, rr]] = 1.0
    c = np.arange(GRID_W)[:, None]
    d = np.arange(GRID_W)[None, :]
    cs = np.clip(c - NA_COLS // 2, 0, GRID_W - NA_COLS)
    cc, dd = np.nonzero((d >= cs) & (d < cs + NA_COLS))
    col_sel = np.zeros((GRID_W, GRID_W, 2 * NA_COLS - 1), np.float32)
    col_sel[cc, dd, (d - c + NA_COLS - 1)[cc, dd]] = 1.0
    return row_sel, col_sel


def _na_bias(rpb, rows):
    row_sel, col_sel = _na_geometry(rows)
    bias = jnp.einsum('vari,hij,cdj->vhacrd', jnp.asarray(row_sel), rpb, jnp.asarray(col_sel),
                      precision=lax.Precision.HIGHEST)
    valid = ((row_sel.sum(-1) > 0)[:, None, :, None, :, None]
             & (col_sel.sum(-1) > 0)[None, None, None, :, None, :])
    bias = jnp.where(jnp.asarray(valid), bias, NEG_BIG)
    return bias.reshape(3, rpb.shape[0], NA_TQ, 3 * NA_TQ)


def _rope(x, cos, sin, first):
    partner = jnp.where(first, pltpu.roll(x, 3 * HEAD_DIM // 4, 1), pltpu.roll(x, HEAD_DIM // 4, 1))
    return x * cos + partner * sin


def _na_body(q_ref, kp_ref, kc_ref, kn_ref, vp_ref, vc_ref, vn_ref, kx_ref, vx_ref,
             cp_ref, cc_ref, cn_ref, sp_ref, sc_ref, sn_ref, bias_ref, o_ref):
    scale = HEAD_DIM ** -0.5
    nt = (((1,), (1,)), ((), ()))
    lane = lax.broadcasted_iota(jnp.int32, (NA_TQ, HEAD_DIM), 1)
    first = (lane % (HEAD_DIM // 2)) < (HEAD_DIM // 4)
    q = q_ref[...]
    qr = _rope(q, cc_ref[...], sc_ref[...], first).astype(BF16)
    kr = jnp.concatenate([
        _rope(kp_ref[...], cp_ref[...], sp_ref[...], first).astype(BF16),
        _rope(kc_ref[...], cc_ref[...], sc_ref[...], first).astype(BF16),
        _rope(kn_ref[...], cn_ref[...], sn_ref[...], first).astype(BF16)], axis=0)
    s_lat = lax.dot_general(qr, kr, nt, preferred_element_type=F32) * scale + bias_ref[0, 0]
    s_ctx = lax.dot_general(q.astype(BF16), kx_ref[...].astype(BF16), nt,
                            preferred_element_type=F32) * scale
    m = jnp.maximum(jnp.max(s_lat, axis=-1, keepdims=True), jnp.max(s_ctx, axis=-1, keepdims=True))
    p_lat = jnp.exp(s_lat - m)
    p_ctx = jnp.exp(s_ctx - m)
    denom = jnp.sum(p_lat, axis=-1, keepdims=True) + jnp.sum(p_ctx, axis=-1, keepdims=True)
    v_all = jnp.concatenate([vp_ref[...].astype(BF16), vc_ref[...].astype(BF16),
                             vn_ref[...].astype(BF16)], axis=0)
    o = (jnp.dot(p_lat.astype(BF16), v_all, preferred_element_type=F32)
         + jnp.dot(p_ctx.astype(BF16), vx_ref[...].astype(BF16), preferred_element_type=F32))
    o_ref[...] = (o / denom).astype(o_ref.dtype)


def neighbourhood_attention(p, pc, bias, cos_tab, sin_tab, *, batch, n_heads, col_q, ctx_len):
    n_tok = p.shape[0] // batch
    n_tiles = n_tok // NA_TQ
    col_k, col_v = col_q + n_heads, col_q + 2 * n_heads
    last = n_tiles - 1

    def tile(h, b, t):
        return b * n_tiles + t

    def prev(h, b, t):
        return b * n_tiles + jnp.maximum(t - 1, 0)

    def nxt(h, b, t):
        return b * n_tiles + jnp.minimum(t + 1, last)

    def variant(h, b, t):
        return jnp.where(t == 0, 0, jnp.where(t == last, 2, 1))

    blk = (NA_TQ, HEAD_DIM)
    ctx_blk = (ctx_len, HEAD_DIM)
    in_specs = [
        pl.BlockSpec(blk, lambda h, b, t: (tile(h, b, t), col_q + h)),
        pl.BlockSpec(blk, lambda h, b, t: (prev(h, b, t), col_k + h)),
        pl.BlockSpec(blk, lambda h, b, t: (tile(h, b, t), col_k + h)),
        pl.BlockSpec(blk, lambda h, b, t: (nxt(h, b, t), col_k + h)),
        pl.BlockSpec(blk, lambda h, b, t: (prev(h, b, t), col_v + h)),
        pl.BlockSpec(blk, lambda h, b, t: (tile(h, b, t), col_v + h)),
        pl.BlockSpec(blk, lambda h, b, t: (nxt(h, b, t), col_v + h)),
        pl.BlockSpec(ctx_blk, lambda h, b, t: (b, col_k + h)),
        pl.BlockSpec(ctx_blk, lambda h, b, t: (b, col_v + h)),
        pl.BlockSpec(blk, lambda h, b, t: (jnp.maximum(t - 1, 0), 0)),
        pl.BlockSpec(blk, lambda h, b, t: (t, 0)),
        pl.BlockSpec(blk, lambda h, b, t: (jnp.minimum(t + 1, last), 0)),
        pl.BlockSpec(blk, lambda h, b, t: (jnp.maximum(t - 1, 0), 0)),
        pl.BlockSpec(blk, lambda h, b, t: (t, 0)),
        pl.BlockSpec(blk, lambda h, b, t: (jnp.minimum(t + 1, last), 0)),
        pl.BlockSpec((1, 1, NA_TQ, 3 * NA_TQ), lambda h, b, t: (variant(h, b, t), h, 0, 0)),
    ]
    return pl.pallas_call(
        _na_body,
        grid=(n_heads, batch, n_tiles),
        in_specs=in_specs,
        out_specs=pl.BlockSpec(blk, lambda h, b, t: (tile(h, b, t), h)),
        out_shape=jax.ShapeDtypeStruct((p.shape[0], n_heads * HEAD_DIM), BF16),
        compiler_params=_cparams("parallel", "parallel", "arbitrary"),
        name="neighbourhood_attention",
    )(p, p, p, p, p, p, p, pc, pc, cos_tab, cos_tab, cos_tab, sin_tab, sin_tab, sin_tab, bias)


def _ctx_attn_body(q_ref, k_ref, v_ref, o_ref):
    nt = (((1,), (1,)), ((), ()))
    s = lax.dot_general(q_ref[...].astype(BF16), k_ref[...].astype(BF16), nt,
                        preferred_element_type=F32) * HEAD_DIM ** -0.5
    p = jnp.exp(s - jnp.max(s, axis=-1, keepdims=True))
    denom = jnp.sum(p, axis=-1, keepdims=True)
    o = jnp.dot(p.astype(BF16), v_ref[...].astype(BF16), preferred_element_type=F32)
    o_ref[...] = (o / denom).astype(o_ref.dtype)


def context_attention(pc, *, batch, n_heads, col_q, ctx_len):
    col_k, col_v = col_q + n_heads, col_q + 2 * n_heads
    blk = (ctx_len, HEAD_DIM)
    return pl.pallas_call(
        _ctx_attn_body,
        grid=(batch, n_heads),
        in_specs=[pl.BlockSpec(blk, lambda b, h: (b, col_q + h)),
                  pl.BlockSpec(blk, lambda b, h: (b, col_k + h)),
                  pl.BlockSpec(blk, lambda b, h: (b, col_v + h))],
        out_specs=pl.BlockSpec(blk, lambda b, h: (b, h)),
        out_shape=jax.ShapeDtypeStruct((pc.shape[0], n_heads * HEAD_DIM), BF16),
        compiler_params=_cparams("parallel", "parallel"),
        name="context_attention",
    )(pc, pc, pc)


_NT = (((1,), (1,)), ((), ()))
_TN = (((0,), (0,)), ((), ()))


def _hg_gates(zq, zf, lb):
    q = zq * jax.nn.sigmoid(zq) * HEAD_DIM ** -0.5
    f = lb + (1.0 - lb) * jax.nn.sigmoid(zf)
    log_f = jnp.log(jnp.maximum(f, F_MIN))
    k = (1.0 - lb) * jax.nn.sigmoid(-zf)
    return q, k, log_f


def _block_cumsum(tri, g):
    out = []
    for r in range(g.shape[0] // tri.shape[0]):
        x = g[r * tri.shape[0]:(r + 1) * tri.shape[0], :]
        hi = x.astype(BF16)
        rest = x - hi.astype(F32)
        mid = rest.astype(BF16)
        lo = (rest - mid.astype(F32)).astype(BF16)
        out.append(jnp.dot(tri, hi, preferred_element_type=F32)
                   + jnp.dot(tri, mid, preferred_element_type=F32)
                   + jnp.dot(tri, lo, preferred_element_type=F32))
    return jnp.concatenate(out, axis=0)


def _hg_chunk(r, d, st, qe_s, k_s, b_s, v_ref, o_ref, reverse):
    L = HG_CHUNK
    qe = qe_s[d, r:r + L, :]
    ke = k_s[d, r:r + L, :]
    vI = v_ref[r:r + L, :].astype(BF16)
    att = lax.dot_general(qe, ke.astype(BF16), _NT, preferred_element_type=F32)
    ti = lax.broadcasted_iota(jnp.int32, (L, L), 0)
    si = lax.broadcasted_iota(jnp.int32, (L, L), 1)
    att = jnp.where((si >= ti) if reverse else (si <= ti), att, 0.0)
    o_ref[r:r + L, :] = (jnp.dot(att.astype(BF16), vI, preferred_element_type=F32)
                         + lax.dot_general(qe, st.astype(BF16), _NT, preferred_element_type=F32))
    end = r if reverse else r + L - 1
    e_end = jnp.exp(b_s[d, end:end + 1, :])
    upd = lax.dot_general(vI, (ke * e_end).astype(BF16), _TN, preferred_element_type=F32)
    return st * e_end + upd


def _hg_sub_block(r, d, q_s, k_s, b_s, v_ref, o_ref, st_s, reverse):
    nt, tn = _NT, _TN
    bI = b_s[d, pl.ds(r, HG_SUB), :]
    qI = q_s[d, pl.ds(r, HG_SUB), :]
    kI = k_s[d, pl.ds(r, HG_SUB), :]
    vI = v_ref[pl.ds(r, HG_SUB), :]
    st = st_s[d]
    o_inter = lax.dot_general((qI * jnp.exp(bI)).astype(BF16), st.astype(BF16), nt,
                              preferred_element_type=F32)
    sub = lax.broadcasted_iota(jnp.int32, (HG_SUB, HEAD_DIM), 0)
    slabs = []
    for t in range(HG_SUB):
        keep = (sub >= t) if reverse else (sub <= t)
        decay = jnp.where(keep, jnp.exp(bI[t:t + 1, :] - bI), 0.0)
        slabs.append(decay * (qI[t:t + 1, :] * kI))
    x3 = jnp.concatenate(slabs, axis=0).astype(BF16)
    att = jnp.dot(x3, jnp.ones((HEAD_DIM, HEAD_DIM), BF16), preferred_element_type=F32)
    z = att * jnp.concatenate([vI] * HG_SUB, axis=0)
    o_intra = jnp.sum(z.reshape(HG_SUB, HG_SUB, HEAD_DIM), axis=1)
    o_ref[pl.ds(r, HG_SUB), :] = o_inter + o_intra
    b_end = bI[0:1, :] if reverse else bI[HG_SUB - 1:HG_SUB, :]
    k_dec = (kI * jnp.exp(b_end - bI)).astype(BF16)
    upd = lax.dot_general(vI.astype(BF16), k_dec, tn, preferred_element_type=F32)
    st_s[d] = st * jnp.exp(b_end) + upd


def _hg_body(zqf_ref, zvf_ref, zff_ref, zqb_ref, zvb_ref, zfb_ref, lbf_ref, lbb_ref,
             tri_ref, s0_ref, of_ref, ob_ref, sout_ref,
             q_s, k_s, g_s, b_s, qe_s, st_s, *, tb):
    c = pl.program_id(1)

    @pl.when(c == 0)
    def _():
        st_s[...] = s0_ref[...]

    lowest = None
    for d, (zq_ref, zf_ref, lb_ref) in enumerate(((zqf_ref, zff_ref, lbf_ref), (zqb_ref, zfb_ref, lbb_ref))):
        q, k, g = _hg_gates(zq_ref[...], zf_ref[...], lb_ref[...])
        b = _block_cumsum(tri_ref[d], g)
        q_s[d] = q
        k_s[d] = k
        g_s[d] = g
        b_s[d] = b
        lowest = jnp.min(b) if lowest is None else jnp.minimum(lowest, jnp.min(b))
    safe = lowest >= -HG_SAFE_DECAY

    @pl.when(safe)
    def _():
        for d in range(2):
            b = b_s[d]
            qe_s[d] = (q_s[d] * jnp.exp(b)).astype(BF16)
            k_s[d] = k_s[d] * jnp.exp(-b)
        n = tb // HG_CHUNK
        st_f, st_b = st_s[0], st_s[1]
        for i in range(n):
            st_f = _hg_chunk(i * HG_CHUNK, 0, st_f, qe_s, k_s, b_s, zvf_ref, of_ref, False)
            st_b = _hg_chunk((n - 1 - i) * HG_CHUNK, 1, st_b, qe_s, k_s, b_s, zvb_ref, ob_ref, True)
        st_s[0] = st_f
        st_s[1] = st_b

    @pl.when(jnp.logical_not(safe))
    def _():
        for d in range(2):
            b_s[d] = _block_cumsum(tri_ref[2 + d], g_s[d])
        n = tb // HG_SUB

        def step(i, carry):
            rf = pl.multiple_of(i * HG_SUB, HG_SUB)
            _hg_sub_block(rf, 0, q_s, k_s, b_s, zvf_ref, of_ref, st_s, False)
            rb = pl.multiple_of((n - 1 - i) * HG_SUB, HG_SUB)
            _hg_sub_block(rb, 1, q_s, k_s, b_s, zvb_ref, ob_ref, st_s, True)
            return carry

        lax.fori_loop(0, n, step, 0)

    @pl.when(c == pl.num_programs(1) - 1)
    def _():
        sout_ref[...] = st_s[...]


def hgrn2_bidir(p, lb_f, lb_b, s0, *, batch, n_heads, col0, tb):
    n_tok = p.shape[0] // batch
    nb = n_tok // tb
    H = n_heads
    tri = []
    for size in (HG_CHUNK, HG_SUB):
        eye = np.kron(np.eye(HG_TRI // size), np.ones((size, size)))
        tri += [np.tril(eye), np.triu(eye)]
    tri = jnp.asarray(np.stack(tri), BF16)

    def fwd(col):
        return pl.BlockSpec((tb, HEAD_DIM), lambda bh, c: ((bh // H) * nb + c, col + bh % H))

    def bwd(col):
        return pl.BlockSpec((tb, HEAD_DIM), lambda bh, c: ((bh // H) * nb + nb - 1 - c, col + bh % H))

    def out(rev):
        if rev:
            return pl.BlockSpec((tb, HEAD_DIM), lambda bh, c: ((bh // H) * nb + nb - 1 - c, bh % H))
        return pl.BlockSpec((tb, HEAD_DIM), lambda bh, c: ((bh // H) * nb + c, bh % H))

    lb_spec = pl.BlockSpec((None, 1, HEAD_DIM), lambda bh, c: (bh % H, 0, 0))
    tri_spec = pl.BlockSpec((4, HG_TRI, HG_TRI), lambda bh, c: (0, 0, 0))
    st_spec = pl.BlockSpec((None, 2, HEAD_DIM, HEAD_DIM), lambda bh, c: (bh, 0, 0, 0))
    o_shape = jax.ShapeDtypeStruct((p.shape[0], H * HEAD_DIM), F32)
    dir_buf = pltpu.VMEM((2, tb, HEAD_DIM), F32)
    return pl.pallas_call(
        functools.partial(_hg_body, tb=tb),
        grid=(batch * H, nb),
        in_specs=[fwd(col0), fwd(col0 + H), fwd(col0 + 2 * H),
                  bwd(col0), bwd(col0 + H), bwd(col0 + 3 * H),
                  lb_spec, lb_spec, tri_spec, st_spec],
        out_specs=[out(False), out(True), st_spec],
        out_shape=[o_shape, o_shape, jax.ShapeDtypeStruct(s0.shape, F32)],
        scratch_shapes=[dir_buf, dir_buf, dir_buf, dir_buf, pltpu.VMEM((2, tb, HEAD_DIM), BF16),
                        pltpu.VMEM((2, HEAD_DIM, HEAD_DIM), F32)],
        compiler_params=_cparams("parallel", "arbitrary"),
        name="hgrn2_bidir",
    )(p, p, p, p, p, p, lb_f, lb_b, tri, s0)


def _hg_out_body(of_ref, ob_ref, g_ref, gain_ref, y_ref):
    o = of_ref[...] + ob_ref[...]
    on = o * lax.rsqrt(jnp.mean(o * o, axis=-1, keepdims=True) + NORM_EPS) * gain_ref[...]
    g = g_ref[...]
    y_ref[...] = (on * (g * jax.nn.sigmoid(g))).astype(y_ref.dtype)


def hgrn2_output(o_f, o_b, p, gain, *, n_heads, col_g, tm):
    M = o_f.shape[0]
    blk = (tm, HEAD_DIM)
    return pl.pallas_call(
        _hg_out_body,
        grid=(M // tm, n_heads),
        in_specs=[pl.BlockSpec(blk, lambda i, h: (i, h)),
                  pl.BlockSpec(blk, lambda i, h: (i, h)),
                  pl.BlockSpec(blk, lambda i, h: (i, col_g + h)),
                  pl.BlockSpec((None, 1, HEAD_DIM), lambda i, h: (h, 0, 0))],
        out_specs=pl.BlockSpec(blk, lambda i, h: (i, h)),
        out_shape=jax.ShapeDtypeStruct(o_f.shape, BF16),
        compiler_params=_cparams("parallel", "parallel"),
        name="hgrn2_output",
    )(o_f, o_b, p, gain)


def _ffn_body(x_ref, wg_ref, wu_ref, wd_ref, gate_ref, o_ref):
    x = x_ref[...]
    u = jnp.dot(x, wg_ref[...], preferred_element_type=F32)
    w = jnp.dot(x, wu_ref[...], preferred_element_type=F32)
    a = (u * jax.nn.sigmoid(u) * w).astype(BF16)
    o_ref[...] = jnp.dot(a, wd_ref[...], preferred_element_type=F32) * gate_ref[...]


def expert_ffn(xe, w_gate, w_up, w_down, gate, *, tm):
    B, E, cap, D = xe.shape
    F = w_gate.shape[-1]
    return pl.pallas_call(
        _ffn_body,
        grid=(E, B, cap // tm),
        in_specs=[pl.BlockSpec((None, None, tm, D), lambda e, b, r: (b, e, r, 0)),
                  pl.BlockSpec((None, D, F), lambda e, b, r: (e, 0, 0)),
                  pl.BlockSpec((None, D, F), lambda e, b, r: (e, 0, 0)),
                  pl.BlockSpec((None, F, D), lambda e, b, r: (e, 0, 0)),
                  pl.BlockSpec((None, None, tm, 1), lambda e, b, r: (b, e, r, 0))],
        out_specs=pl.BlockSpec((None, None, tm, D), lambda e, b, r: (b, e, r, 0)),
        out_shape=jax.ShapeDtypeStruct((B, E, cap, D), F32),
        compiler_params=_cparams("parallel", "parallel", "arbitrary"),
        name="expert_ffn",
    )(xe, w_gate, w_up, w_down, gate)


def _conformer_conv(a, w_dw, b_dw, ln_g, ln_b):
    w_conv = a.shape[-1] // 2
    u = a[..., :w_conv] * jax.nn.sigmoid(a[..., w_conv:])
    u = lax.conv_general_dilated(
        u, w_dw[:, None, :], window_strides=(1,), padding=[(CONV_K // 2, CONV_K // 2)],
        dimension_numbers=('NWC', 'WIO', 'NWC'), feature_group_count=w_conv) + b_dw
    mu = jnp.mean(u, axis=-1, keepdims=True)
    var = jnp.mean(jnp.square(u - mu), axis=-1, keepdims=True)
    un = (u - mu) * lax.rsqrt(var + NORM_EPS) * ln_g + ln_b
    return jax.nn.silu(un).astype(BF16)


def _expert_choice_ffn(h, router, w_gate, w_up, w_down):
    B, n, D = h.shape
    cap = EC_CAPACITY * n // N_EXPERTS
    logits = jnp.dot(h, router, precision=lax.Precision.HIGHEST)
    aff = jax.nn.softmax(logits, axis=-1)
    gate, idx = lax.top_k(aff.transpose(0, 2, 1), cap)
    bi = jnp.arange(B)[:, None, None]
    xe = h.astype(BF16)[bi, idx]
    y = expert_ffn(xe, w_gate, w_up, w_down, gate[..., None], tm=min(cap, 512))
    return jnp.zeros_like(h).at[bi, idx].add(y)


def kernel(x, c, ctx, c_ctx, mod_down, mod_up, mod_bias, mix_norm, ffn_norm, w_in, w_out,
           conv_w, conv_b, conv_ln_g, conv_ln_b, na_rpb, hg_lb_logits, hg_norm, router,
           w_gate, w_up, w_down, final_norm):
    B, N, D = x.shape
    M = ctx.shape[1]
    depth = w_in.shape[0]
    rows = N // GRID_W
    w_conv = conv_w.shape[-1]
    w_hg = hg_norm.shape[-1]
    n_hg = w_hg // HEAD_DIM
    n_na = na_rpb.shape[1]
    w_na = n_na * HEAD_DIM
    in_conv = 2 * w_conv
    col_na = in_conv // HEAD_DIM
    col_hg = (in_conv + 3 * w_na) // HEAD_DIM

    lb_p = jax.nn.softmax(hg_lb_logits.astype(F32), axis=1)
    lb_all = jnp.cumsum(lb_p, axis=1) - lb_p[:, :1]
    a_lat = jax.nn.silu(c)
    a_ctx = jax.nn.silu(c_ctx)
    cos_tab, sin_tab = _rope_tables(N)
    s_zero = jnp.zeros((B * n_hg, 2, HEAD_DIM, HEAD_DIM), F32)

    xl = x.reshape(B * N, D)
    xc = ctx.reshape(B * M, D)
    for l in range(depth):
        last = l == depth - 1
        hi = lax.Precision.HIGHEST
        mod = jnp.dot(jnp.dot(a_lat, mod_down[l], precision=hi), mod_up[l], precision=hi) + mod_bias[l]
        mod_c = jnp.dot(jnp.dot(a_ctx, mod_down[l], precision=hi), mod_up[l], precision=hi) + mod_bias[l]
        sh1, sc1, g1, sh2, sc2, g2 = [m[:, None, :] for m in jnp.split(mod, N_MOD, axis=-1)]
        csh1, csc1, cg1, csh2, csc2, cg2 = [m[None, None, :] for m in jnp.split(mod_c, N_MOD, axis=-1)]

        w_in_l = w_in[l].astype(BF16)
        wo = w_out[l]
        wo_a = wo[:w_conv].astype(BF16)
        wo_b = wo[w_conv:w_conv + w_na].astype(BF16)
        wo_c = wo[w_conv + w_na:].astype(BF16)
        wg_l, wu_l, wd_l = w_gate[l].astype(BF16), w_up[l].astype(BF16), w_down[l].astype(BF16)
        g_mix = mix_norm[l][None, :]
        g_ffn = ffn_norm[l][None, :]

        tn_in = _tile(w_in_l.shape[1], 1024)
        tn_out = _tile(D, 1024)
        p = norm_mod_matmul(xl, g_mix, sc1, sh1, w_in_l, tm=512, tn=tn_in)
        pc = norm_mod_matmul(xc, g_mix, csc1, csh1, w_in_l, tm=B * M, tn=tn_in)

        ya = _conformer_conv(p[:, :in_conv].reshape(B, N, in_conv), conv_w[l], conv_b[l],
                             conv_ln_g[l], conv_ln_b[l]).reshape(B * N, w_conv)
        bias = _na_bias(na_rpb[l], rows)
        yb = neighbourhood_attention(p, pc, bias, cos_tab, sin_tab, batch=B, n_heads=n_na,
                                     col_q=col_na, ctx_len=M)
        lb_f = lb_all[0, l].reshape(n_hg, 1, HEAD_DIM)
        lb_b = lb_all[1, l].reshape(n_hg, 1, HEAD_DIM)
        oc_f, oc_b, s_ctx = hgrn2_bidir(pc, lb_f, lb_b, s_zero, batch=B, n_heads=n_hg, col0=col_hg, tb=M)
        o_f, o_b, _ = hgrn2_bidir(p, lb_f, lb_b, s_ctx, batch=B, n_heads=n_hg, col0=col_hg, tb=HG_TB)
        gain = hg_norm[l].reshape(n_hg, 1, HEAD_DIM)
        yc = hgrn2_output(o_f, o_b, p, gain, n_heads=n_hg, col_g=col_hg + 4 * n_hg, tm=512)

        xl = outproj_residual(ya, yb, yc, wo_a, wo_b, wo_c, xl, g1, tm=512, tn=tn_out)
        h2 = norm_mod(xl, g_ffn, sc2, sh2, tm=512)
        moe = _expert_choice_ffn(h2.reshape(B, N, D), router[l], wg_l, wu_l, wd_l)
        xl = xl + (g2 * moe).reshape(B * N, D)

        if not last:
            yac = _conformer_conv(pc[:, :in_conv].reshape(B, M, in_conv), conv_w[l], conv_b[l],
                                  conv_ln_g[l], conv_ln_b[l]).reshape(B * M, w_conv)
            ybc = context_attention(pc, batch=B, n_heads=n_na, col_q=col_na, ctx_len=M)
            ycc = hgrn2_output(oc_f, oc_b, pc, gain, n_heads=n_hg, col_g=col_hg + 4 * n_hg, tm=B * M)
            xc = outproj_residual(yac, ybc, ycc, wo_a, wo_b, wo_c, xc, cg1, tm=B * M, tn=tn_out)
            hc2 = norm_mod(xc, g_ffn, csc2, csh2, tm=B * M)
            moe_c = _expert_choice_ffn(hc2.reshape(B, M, D), router[l], wg_l, wu_l, wd_l)
            xc = xc + (cg2 * moe_c).reshape(B * M, D)

    ones = jnp.ones((1, 1, D), F32)
    out = norm_mod(xl, final_norm[None, :], ones * 0.0, ones * 0.0, tm=512)
    return out.reshape(B, N, D)
```

```python
import functools

import numpy as np
import jax
import jax.numpy as jnp
from jax import lax
from jax.experimental import pallas as pl
from jax.experimental.pallas import tpu as pltpu

GRID_W = 64
HEAD_DIM = 128
CONV_K = 31
NA_ROWS = 8
NA_COLS = 16
ROPE_THETA = 10000.0
F_MIN = 1e-6
N_EXPERTS = 16
EC_CAPACITY = 2
N_MOD = 6
NORM_EPS = 1e-6

BF16 = jnp.bfloat16
F32 = jnp.float32

VMEM_LIMIT_BYTES = 56 * 1024 * 1024
NEG_BIG = -1e30
NA_TILE_ROWS = 4
NA_TQ = NA_TILE_ROWS * GRID_W
SUBLANES = 8
CONV_HALO = 16
CONV_ROWS = 32
ROUTER_PAD = 128
HG_SUB = 16
HG_CHUNK = 32
HG_SAFE_DECAY = 80.0
HG_TRI = 128
HG_TB = 512


def _cparams(*sem):
    return pltpu.CompilerParams(dimension_semantics=sem, vmem_limit_bytes=VMEM_LIMIT_BYTES)


def _tile(n, preferred):
    t = preferred
    while n % t:
        t //= 2
    return t


def _norm_mod(x, g, sc, sh):
    r = lax.rsqrt(jnp.mean(x * x, axis=-1, keepdims=True) + NORM_EPS)
    return (x * r * g) * (1.0 + sc) + sh


def _nm_body(x_ref, g_ref, sc_ref, sh_ref, *rest):
    h = _norm_mod(x_ref[...], g_ref[...], sc_ref[0], sh_ref[0])
    if len(rest) == 3:
        r_ref, o_ref, logit_ref = rest
        logit_ref[...] = jnp.dot(h, r_ref[...], preferred_element_type=F32, precision=lax.Precision.HIGHEST)
    else:
        o_ref, = rest
    o_ref[...] = h.astype(o_ref.dtype)


def norm_mod(x, g, sc, sh, *, tm, router=None, out_dtype=BF16):
    M, D = x.shape
    G = sc.shape[0]
    tiles_per_group = M // G // tm
    in_specs = [
        pl.BlockSpec((tm, D), lambda i: (i, 0)),
        pl.BlockSpec((1, D), lambda i: (0, 0)),
        pl.BlockSpec((1, 1, D), lambda i: (i // tiles_per_group, 0, 0)),
        pl.BlockSpec((1, 1, D), lambda i: (i // tiles_per_group, 0, 0)),
    ]
    out_specs = pl.BlockSpec((tm, D), lambda i: (i, 0))
    out_shape = jax.ShapeDtypeStruct((M, D), out_dtype)
    args = (x, g, sc, sh)
    if router is not None:
        E = router.shape[1]
        in_specs.append(pl.BlockSpec((D, E), lambda i: (0, 0)))
        out_specs = [out_specs, pl.BlockSpec((tm, E), lambda i: (i, 0))]
        out_shape = [out_shape, jax.ShapeDtypeStruct((M, E), F32)]
        args += (router,)
    return pl.pallas_call(
        _nm_body,
        grid=(M // tm,),
        in_specs=in_specs,
        out_specs=out_specs,
        out_shape=out_shape,
        compiler_params=_cparams("parallel"),
        name="norm_mod",
    )(*args)


def _mm_body(h_ref, w_ref, o_ref, wb_scr):
    @pl.when(pl.program_id(1) == 0)
    def _():
        wb_scr[...] = w_ref[...].astype(BF16)

    o_ref[...] = jnp.dot(h_ref[...], wb_scr[...], preferred_element_type=F32)


def matmul_wcast(h, w, *, tm, tn):
    M, K = h.shape
    N = w.shape[1]
    return pl.pallas_call(
        _mm_body,
        grid=(N // tn, M // tm),
        in_specs=[pl.BlockSpec((tm, K), lambda j, i: (i, 0)),
                  pl.BlockSpec((K, tn), lambda j, i: (0, j))],
        out_specs=pl.BlockSpec((tm, tn), lambda j, i: (i, j)),
        out_shape=jax.ShapeDtypeStruct((M, N), F32),
        scratch_shapes=[pltpu.VMEM((K, tn), BF16)],
        compiler_params=_cparams("parallel", "arbitrary"),
        name="matmul_wcast",
    )(h, w)


def _outproj_body(ya_ref, yb_ref, yc_ref, w_ref, x_ref, g_ref, o_ref, wb_scr):
    @pl.when(pl.program_id(1) == 0)
    def _():
        wb_scr[...] = w_ref[...].astype(BF16)

    ka, kb = ya_ref.shape[1], yb_ref.shape[1]
    acc = jnp.dot(ya_ref[...], wb_scr[0:ka, :], preferred_element_type=F32)
    acc += jnp.dot(yb_ref[...], wb_scr[ka:ka + kb, :], preferred_element_type=F32)
    acc += jnp.dot(yc_ref[...], wb_scr[ka + kb:, :], preferred_element_type=F32)
    o_ref[...] = x_ref[...] + g_ref[0] * acc


def outproj_residual(ya, yb, yc, w, x, gate, *, tm, tn):
    M, D = x.shape
    G = gate.shape[0]
    tiles_per_group = M // G // tm
    ka, kb, kc = ya.shape[1], yb.shape[1], yc.shape[1]
    K = ka + kb + kc
    return pl.pallas_call(
        _outproj_body,
        grid=(D // tn, M // tm),
        in_specs=[
            pl.BlockSpec((tm, ka), lambda j, i: (i, 0)),
            pl.BlockSpec((tm, kb), lambda j, i: (i, 0)),
            pl.BlockSpec((tm, kc), lambda j, i: (i, 0)),
            pl.BlockSpec((K, tn), lambda j, i: (0, j)),
            pl.BlockSpec((tm, tn), lambda j, i: (i, j)),
            pl.BlockSpec((1, 1, tn), lambda j, i: (i // tiles_per_group, 0, j)),
        ],
        out_specs=pl.BlockSpec((tm, tn), lambda j, i: (i, j)),
        out_shape=jax.ShapeDtypeStruct((M, D), F32),
        scratch_shapes=[pltpu.VMEM((K, tn), BF16)],
        compiler_params=_cparams("parallel", "arbitrary"),
        name="outproj_residual",
    )(ya, yb, yc, w, x, gate)


def _rope_tables(n_tokens):
    half = HEAD_DIM // 2
    n_freq = half // 2
    t = np.arange(n_tokens)
    pos = np.stack([t // GRID_W, t % GRID_W], axis=1).astype(np.float32)
    d = np.arange(HEAD_DIM)
    which = d // half
    sign = np.where((d % half) < n_freq, -1.0, 1.0).astype(np.float32)
    inv_freq = ROPE_THETA ** (-jnp.arange(n_freq, dtype=F32) / n_freq)
    ang = jnp.asarray(pos)[:, which] * inv_freq[d % n_freq][None, :]
    return jnp.cos(ang), jnp.sin(ang) * sign[None, :]


def _na_geometry(rows):
    n_tiles = rows // NA_TILE_ROWS
    kr = min(NA_ROWS, rows)
    a = np.arange(NA_TILE_ROWS)[:, None]
    r = np.arange(3 * NA_TILE_ROWS)[None, :]
    row_sel = np.zeros((3, NA_TILE_ROWS, 3 * NA_TILE_ROWS, 2 * NA_ROWS - 1), np.float32)
    for v, t in enumerate((0, min(1, n_tiles - 1), n_tiles - 1)):
        qrow = NA_TILE_ROWS * t + a
        krow = NA_TILE_ROWS * (t - 1) + r
        rs = np.clip(qrow - kr // 2, 0, rows - kr)
        ok = (krow >= rs) & (krow < rs + kr) & (krow >= 0) & (krow < rows)
        aa, rr = np.nonzero(ok)
        row_sel[v, aa, rr, (krow - qrow + NA_ROWS - 1)[aa, rr]] = 1.0
    c = np.arange(GRID_W)[:, None]
    d = np.arange(GRID_W)[None, :]
    cs = np.clip(c - NA_COLS // 2, 0, GRID_W - NA_COLS)
    cc, dd = np.nonzero((d >= cs) & (d < cs + NA_COLS))
    col_sel = np.zeros((GRID_W, GRID_W, 2 * NA_COLS - 1), np.float32)
    col_sel[cc, dd, (d - c + NA_COLS - 1)[cc, dd]] = 1.0
    return row_sel, col_sel


def _na_bias(rpb, rows):
    row_sel, col_sel = _na_geometry(rows)
    bias = jnp.einsum('vari,hij,cdj->vhacrd', jnp.asarray(row_sel), rpb, jnp.asarray(col_sel),
                      precision=lax.Precision.HIGHEST)
    valid = ((row_sel.sum(-1) > 0)[:, None, :, None, :, None]
             & (col_sel.sum(-1) > 0)[None, None, None, :, None, :])
    bias = jnp.where(jnp.asarray(valid), bias, NEG_BIG)
    return bias.reshape(3, rpb.shape[0], NA_TQ, 3 * NA_TQ)


def _rope(x, cos, sin, first):
    partner = jnp.where(first, pltpu.roll(x, 3 * HEAD_DIM // 4, 1), pltpu.roll(x, HEAD_DIM // 4, 1))
    return x * cos + partner * sin


def _rope_kv_body(k_ref, v_ref, cos_ref, sin_ref, kr_ref, vb_ref):
    lane = lax.broadcasted_iota(jnp.int32, k_ref.shape, 1)
    first = (lane % (HEAD_DIM // 2)) < (HEAD_DIM // 4)
    kr_ref[...] = _rope(k_ref[...], cos_ref[...], sin_ref[...], first).astype(BF16)
    vb_ref[...] = v_ref[...].astype(BF16)


def rope_kv(p, cos_tab, sin_tab, *, batch, n_heads, col_k, tq):
    tiles = p.shape[0] // batch // tq
    blk = (tq, HEAD_DIM)
    shape = jax.ShapeDtypeStruct((p.shape[0], n_heads * HEAD_DIM), BF16)
    return pl.pallas_call(
        _rope_kv_body,
        grid=(p.shape[0] // tq, n_heads),
        in_specs=[pl.BlockSpec(blk, lambda i, h: (i, col_k + h)),
                  pl.BlockSpec(blk, lambda i, h: (i, col_k + n_heads + h)),
                  pl.BlockSpec(blk, lambda i, h: (i % tiles, 0)),
                  pl.BlockSpec(blk, lambda i, h: (i % tiles, 0))],
        out_specs=[pl.BlockSpec(blk, lambda i, h: (i, h)), pl.BlockSpec(blk, lambda i, h: (i, h))],
        out_shape=[shape, shape],
        compiler_params=_cparams("parallel", "arbitrary"),
        name="rope_kv",
    )(p, p, cos_tab, sin_tab)


def _na_body(q_ref, kp_ref, kc_ref, kn_ref, vp_ref, vc_ref, vn_ref, kx_ref, vx_ref,
             cos_ref, sin_ref, bias_ref, o_ref):
    scale = HEAD_DIM ** -0.5
    lane = lax.broadcasted_iota(jnp.int32, (NA_TQ, HEAD_DIM), 1)
    first = (lane % (HEAD_DIM // 2)) < (HEAD_DIM // 4)
    q = q_ref[...]
    qr = _rope(q, cos_ref[...], sin_ref[...], first).astype(BF16)
    kr = jnp.concatenate([kp_ref[...], kc_ref[...], kn_ref[...]], axis=0)
    s_lat = lax.dot_general(qr, kr, _NT, preferred_element_type=F32) * scale + bias_ref[0, 0]
    s_ctx = lax.dot_general(q.astype(BF16), kx_ref[...].astype(BF16), _NT,
                            preferred_element_type=F32) * scale
    m = jnp.maximum(jnp.max(s_lat, axis=-1, keepdims=True), jnp.max(s_ctx, axis=-1, keepdims=True))
    p_lat = jnp.exp(s_lat - m)
    p_ctx = jnp.exp(s_ctx - m)
    denom = jnp.sum(p_lat, axis=-1, keepdims=True) + jnp.sum(p_ctx, axis=-1, keepdims=True)
    v_all = jnp.concatenate([vp_ref[...], vc_ref[...], vn_ref[...]], axis=0)
    o = (jnp.dot(p_lat.astype(BF16), v_all, preferred_element_type=F32)
         + jnp.dot(p_ctx.astype(BF16), vx_ref[...].astype(BF16), preferred_element_type=F32))
    o_ref[...] = (o / denom).astype(o_ref.dtype)


def neighbourhood_attention(p, pc, kr, vb, bias, cos_tab, sin_tab, *, batch, n_heads, col_q, ctx_len):
    n_tok = p.shape[0] // batch
    n_tiles = n_tok // NA_TQ
    col_k, col_v = col_q + n_heads, col_q + 2 * n_heads
    last = n_tiles - 1

    def tile(h, b, t):
        return b * n_tiles + t

    def prev(h, b, t):
        return b * n_tiles + jnp.maximum(t - 1, 0)

    def nxt(h, b, t):
        return b * n_tiles + jnp.minimum(t + 1, last)

    def variant(h, b, t):
        return jnp.where(t == 0, 0, jnp.where(t == last, 2, 1))

    blk = (NA_TQ, HEAD_DIM)
    ctx_blk = (ctx_len, HEAD_DIM)
    in_specs = [
        pl.BlockSpec(blk, lambda h, b, t: (tile(h, b, t), col_q + h)),
        pl.BlockSpec(blk, lambda h, b, t: (prev(h, b, t), h)),
        pl.BlockSpec(blk, lambda h, b, t: (tile(h, b, t), h)),
        pl.BlockSpec(blk, lambda h, b, t: (nxt(h, b, t), h)),
        pl.BlockSpec(blk, lambda h, b, t: (prev(h, b, t), h)),
        pl.BlockSpec(blk, lambda h, b, t: (tile(h, b, t), h)),
        pl.BlockSpec(blk, lambda h, b, t: (nxt(h, b, t), h)),
        pl.BlockSpec(ctx_blk, lambda h, b, t: (b, col_k + h)),
        pl.BlockSpec(ctx_blk, lambda h, b, t: (b, col_v + h)),
        pl.BlockSpec(blk, lambda h, b, t: (t, 0)),
        pl.BlockSpec(blk, lambda h, b, t: (t, 0)),
        pl.BlockSpec((1, 1, NA_TQ, 3 * NA_TQ), lambda h, b, t: (variant(h, b, t), h, 0, 0)),
    ]
    return pl.pallas_call(
        _na_body,
        grid=(n_heads, batch, n_tiles),
        in_specs=in_specs,
        out_specs=pl.BlockSpec(blk, lambda h, b, t: (tile(h, b, t), h)),
        out_shape=jax.ShapeDtypeStruct((p.shape[0], n_heads * HEAD_DIM), BF16),
        compiler_params=_cparams("parallel", "parallel", "arbitrary"),
        name="neighbourhood_attention",
    )(p, kr, kr, kr, vb, vb, vb, pc, pc, cos_tab, sin_tab, bias)


def _ctx_attn_body(q_ref, k_ref, v_ref, o_ref):
    nt = (((1,), (1,)), ((), ()))
    s = lax.dot_general(q_ref[...].astype(BF16), k_ref[...].astype(BF16), nt,
                        preferred_element_type=F32) * HEAD_DIM ** -0.5
    p = jnp.exp(s - jnp.max(s, axis=-1, keepdims=True))
    denom = jnp.sum(p, axis=-1, keepdims=True)
    o = jnp.dot(p.astype(BF16), v_ref[...].astype(BF16), preferred_element_type=F32)
    o_ref[...] = (o / denom).astype(o_ref.dtype)


def context_attention(pc, *, batch, n_heads, col_q, ctx_len):
    col_k, col_v = col_q + n_heads, col_q + 2 * n_heads
    blk = (ctx_len, HEAD_DIM)
    return pl.pallas_call(
        _ctx_attn_body,
        grid=(batch, n_heads),
        in_specs=[pl.BlockSpec(blk, lambda b, h: (b, col_q + h)),
                  pl.BlockSpec(blk, lambda b, h: (b, col_k + h)),
                  pl.BlockSpec(blk, lambda b, h: (b, col_v + h))],
        out_specs=pl.BlockSpec(blk, lambda b, h: (b, h)),
        out_shape=jax.ShapeDtypeStruct((pc.shape[0], n_heads * HEAD_DIM), BF16),
        compiler_params=_cparams("parallel", "parallel"),
        name="context_attention",
    )(pc, pc, pc)


def _conv_body(val_ref, gate_ref, pval_ref, pgate_ref, nval_ref, ngate_ref, w_ref, b_ref, lg_ref, lb_ref,
               o_ref, u_scr, *, tt):
    t = pl.program_id(1)

    def glu(v_ref, g_ref):
        return v_ref[...] * jax.nn.sigmoid(g_ref[...])

    span = tt + 2 * CONV_HALO
    u_scr[0, 0:CONV_HALO, :] = jnp.where(t > 0, glu(pval_ref, pgate_ref), 0.0)
    u_scr[0, CONV_HALO:CONV_HALO + tt, :] = glu(val_ref, gate_ref)
    u_scr[0, CONV_HALO + tt:span, :] = jnp.where(t < pl.num_programs(1) - 1, glu(nval_ref, ngate_ref), 0.0)
    u_scr[0, span:, :] = jnp.zeros((SUBLANES, u_scr.shape[2]), F32)
    for a in range(1, SUBLANES):
        u_scr[a, 0:span, :] = u_scr[0, a:a + span, :]
    off = CONV_HALO - CONV_K // 2

    def tap(r, k):
        start = r + off + k
        a = start % SUBLANES
        return w_ref[k:k + 1, :] * u_scr[a, start - a:start - a + CONV_ROWS, :]

    for r in range(0, tt, CONV_ROWS):
        acc = tap(r, 0)
        for k in range(1, CONV_K):
            acc = acc + tap(r, k)
        acc = acc + b_ref[...]
        mu = jnp.mean(acc, axis=-1, keepdims=True)
        cen = acc - mu
        var = jnp.mean(cen * cen, axis=-1, keepdims=True)
        un = cen * lax.rsqrt(var + NORM_EPS) * lg_ref[...] + lb_ref[...]
        o_ref[r:r + CONV_ROWS, :] = (un * jax.nn.sigmoid(un)).astype(o_ref.dtype)


def conformer_conv(p, w_dw, b_dw, ln_g, ln_b, *, batch, tt):
    C = w_dw.shape[1]
    T = p.shape[0] // batch
    nt = T // tt
    hb = tt // CONV_HALO
    seq_hb = T // CONV_HALO

    def prev_blk(b, t):
        return jnp.maximum(b * seq_hb + t * hb - 1, 0)

    def next_blk(b, t):
        return jnp.minimum(b * seq_hb + (t + 1) * hb, batch * seq_hb - 1)

    main = (tt, C)
    halo = (CONV_HALO, C)
    vec = pl.BlockSpec((1, C), lambda b, t: (0, 0))
    return pl.pallas_call(
        functools.partial(_conv_body, tt=tt),
        grid=(batch, nt),
        in_specs=[pl.BlockSpec(main, lambda b, t: (b * nt + t, 0)),
                  pl.BlockSpec(main, lambda b, t: (b * nt + t, 1)),
                  pl.BlockSpec(halo, lambda b, t: (prev_blk(b, t), 0)),
                  pl.BlockSpec(halo, lambda b, t: (prev_blk(b, t), 1)),
                  pl.BlockSpec(halo, lambda b, t: (next_blk(b, t), 0)),
                  pl.BlockSpec(halo, lambda b, t: (next_blk(b, t), 1)),
                  pl.BlockSpec((CONV_K, C), lambda b, t: (0, 0)), vec, vec, vec],
        out_specs=pl.BlockSpec(main, lambda b, t: (b * nt + t, 0)),
        out_shape=jax.ShapeDtypeStruct((p.shape[0], C), BF16),
        scratch_shapes=[pltpu.VMEM((SUBLANES, tt + 2 * CONV_HALO + SUBLANES, C), F32)],
        compiler_params=_cparams("parallel", "parallel"),
        name="conformer_conv",
    )(p, p, p, p, p, p, w_dw, b_dw, ln_g, ln_b)


_NT = (((1,), (1,)), ((), ()))
_TN = (((0,), (0,)), ((), ()))


def _hg_gates(zq, zf, lb):
    q = zq * jax.nn.sigmoid(zq) * HEAD_DIM ** -0.5
    f = lb + (1.0 - lb) * jax.nn.sigmoid(zf)
    log_f = jnp.log(jnp.maximum(f, F_MIN))
    k = (1.0 - lb) * jax.nn.sigmoid(-zf)
    return q, k, log_f


def _block_cumsum(tri, g):
    out = []
    for r in range(g.shape[0] // tri.shape[0]):
        x = g[r * tri.shape[0]:(r + 1) * tri.shape[0], :]
        hi = x.astype(BF16)
        rest = x - hi.astype(F32)
        mid = rest.astype(BF16)
        lo = (rest - mid.astype(F32)).astype(BF16)
        out.append(jnp.dot(tri, hi, preferred_element_type=F32)
                   + jnp.dot(tri, mid, preferred_element_type=F32)
                   + jnp.dot(tri, lo, preferred_element_type=F32))
    return jnp.concatenate(out, axis=0)


def _hg_chunk(r, d, st, qe_s, k_s, b_s, v_ref, o_ref, reverse):
    L = HG_CHUNK
    qe = qe_s[d, r:r + L, :]
    ke = k_s[d, r:r + L, :]
    vI = v_ref[r:r + L, :].astype(BF16)
    att = lax.dot_general(qe, ke.astype(BF16), _NT, preferred_element_type=F32)
    ti = lax.broadcasted_iota(jnp.int32, (L, L), 0)
    si = lax.broadcasted_iota(jnp.int32, (L, L), 1)
    att = jnp.where((si >= ti) if reverse else (si <= ti), att, 0.0)
    o_ref[r:r + L, :] = (jnp.dot(att.astype(BF16), vI, preferred_element_type=F32)
                         + lax.dot_general(qe, st.astype(BF16), _NT, preferred_element_type=F32))
    end = r if reverse else r + L - 1
    e_end = jnp.exp(b_s[d, end:end + 1, :])
    upd = lax.dot_general(vI, (ke * e_end).astype(BF16), _TN, preferred_element_type=F32)
    return st * e_end + upd


def _hg_sub_block(r, d, q_s, k_s, b_s, v_ref, o_ref, st_s, reverse):
    nt, tn = _NT, _TN
    bI = b_s[d, pl.ds(r, HG_SUB), :]
    qI = q_s[d, pl.ds(r, HG_SUB), :]
    kI = k_s[d, pl.ds(r, HG_SUB), :]
    vI = v_ref[pl.ds(r, HG_SUB), :]
    st = st_s[d]
    o_inter = lax.dot_general((qI * jnp.exp(bI)).astype(BF16), st.astype(BF16), nt,
                              preferred_element_type=F32)
    sub = lax.broadcasted_iota(jnp.int32, (HG_SUB, HEAD_DIM), 0)
    slabs = []
    for t in range(HG_SUB):
        keep = (sub >= t) if reverse else (sub <= t)
        decay = jnp.where(keep, jnp.exp(bI[t:t + 1, :] - bI), 0.0)
        slabs.append(decay * (qI[t:t + 1, :] * kI))
    x3 = jnp.concatenate(slabs, axis=0).astype(BF16)
    att = jnp.dot(x3, jnp.ones((HEAD_DIM, HEAD_DIM), BF16), preferred_element_type=F32)
    z = att * jnp.concatenate([vI] * HG_SUB, axis=0)
    o_intra = jnp.sum(z.reshape(HG_SUB, HG_SUB, HEAD_DIM), axis=1)
    o_ref[pl.ds(r, HG_SUB), :] = o_inter + o_intra
    b_end = bI[0:1, :] if reverse else bI[HG_SUB - 1:HG_SUB, :]
    k_dec = (kI * jnp.exp(b_end - bI)).astype(BF16)
    upd = lax.dot_general(vI.astype(BF16), k_dec, tn, preferred_element_type=F32)
    st_s[d] = st * jnp.exp(b_end) + upd


def _hg_body(zqf_ref, zvf_ref, zff_ref, zqb_ref, zvb_ref, zfb_ref, lbf_ref, lbb_ref,
             tri_ref, s0_ref, of_ref, ob_ref, sout_ref,
             q_s, k_s, g_s, b_s, qe_s, st_s, *, tb):
    c = pl.program_id(1)

    @pl.when(c == 0)
    def _():
        st_s[...] = s0_ref[...]

    lowest = None
    for d, (zq_ref, zf_ref, lb_ref) in enumerate(((zqf_ref, zff_ref, lbf_ref), (zqb_ref, zfb_ref, lbb_ref))):
        q, k, g = _hg_gates(zq_ref[...], zf_ref[...], lb_ref[...])
        b = _block_cumsum(tri_ref[d], g)
        q_s[d] = q
        k_s[d] = k
        g_s[d] = g
        b_s[d] = b
        lowest = jnp.min(b) if lowest is None else jnp.minimum(lowest, jnp.min(b))
    safe = lowest >= -HG_SAFE_DECAY

    @pl.when(safe)
    def _():
        for d in range(2):
            b = b_s[d]
            qe_s[d] = (q_s[d] * jnp.exp(b)).astype(BF16)
            k_s[d] = k_s[d] * jnp.exp(-b)
        n = tb // HG_CHUNK
        st_f, st_b = st_s[0], st_s[1]
        for i in range(n):
            st_f = _hg_chunk(i * HG_CHUNK, 0, st_f, qe_s, k_s, b_s, zvf_ref, of_ref, False)
            st_b = _hg_chunk((n - 1 - i) * HG_CHUNK, 1, st_b, qe_s, k_s, b_s, zvb_ref, ob_ref, True)
        st_s[0] = st_f
        st_s[1] = st_b

    @pl.when(jnp.logical_not(safe))
    def _():
        for d in range(2):
            b_s[d] = _block_cumsum(tri_ref[2 + d], g_s[d])
        n = tb // HG_SUB

        def step(i, carry):
            rf = pl.multiple_of(i * HG_SUB, HG_SUB)
            _hg_sub_block(rf, 0, q_s, k_s, b_s, zvf_ref, of_ref, st_s, False)
            rb = pl.multiple_of((n - 1 - i) * HG_SUB, HG_SUB)
            _hg_sub_block(rb, 1, q_s, k_s, b_s, zvb_ref, ob_ref, st_s, True)
            return carry

        lax.fori_loop(0, n, step, 0)

    @pl.when(c == pl.num_programs(1) - 1)
    def _():
        sout_ref[...] = st_s[...]


def hgrn2_bidir(p, lb_f, lb_b, s0, *, batch, n_heads, col0, tb):
    n_tok = p.shape[0] // batch
    nb = n_tok // tb
    H = n_heads
    tri = []
    for size in (HG_CHUNK, HG_SUB):
        eye = np.kron(np.eye(HG_TRI // size), np.ones((size, size)))
        tri += [np.tril(eye), np.triu(eye)]
    tri = jnp.asarray(np.stack(tri), BF16)

    def fwd(col):
        return pl.BlockSpec((tb, HEAD_DIM), lambda bh, c: ((bh // H) * nb + c, col + bh % H))

    def bwd(col):
        return pl.BlockSpec((tb, HEAD_DIM), lambda bh, c: ((bh // H) * nb + nb - 1 - c, col + bh % H))

    def out(rev):
        if rev:
            return pl.BlockSpec((tb, HEAD_DIM), lambda bh, c: ((bh // H) * nb + nb - 1 - c, bh % H))
        return pl.BlockSpec((tb, HEAD_DIM), lambda bh, c: ((bh // H) * nb + c, bh % H))

    lb_spec = pl.BlockSpec((None, 1, HEAD_DIM), lambda bh, c: (bh % H, 0, 0))
    tri_spec = pl.BlockSpec((4, HG_TRI, HG_TRI), lambda bh, c: (0, 0, 0))
    st_spec = pl.BlockSpec((None, 2, HEAD_DIM, HEAD_DIM), lambda bh, c: (bh, 0, 0, 0))
    o_shape = jax.ShapeDtypeStruct((p.shape[0], H * HEAD_DIM), F32)
    dir_buf = pltpu.VMEM((2, tb, HEAD_DIM), F32)
    return pl.pallas_call(
        functools.partial(_hg_body, tb=tb),
        grid=(batch * H, nb),
        in_specs=[fwd(col0), fwd(col0 + H), fwd(col0 + 2 * H),
                  bwd(col0), bwd(col0 + H), bwd(col0 + 3 * H),
                  lb_spec, lb_spec, tri_spec, st_spec],
        out_specs=[out(False), out(True), st_spec],
        out_shape=[o_shape, o_shape, jax.ShapeDtypeStruct(s0.shape, F32)],
        scratch_shapes=[dir_buf, dir_buf, dir_buf, dir_buf, pltpu.VMEM((2, tb, HEAD_DIM), BF16),
                        pltpu.VMEM((2, HEAD_DIM, HEAD_DIM), F32)],
        compiler_params=_cparams("parallel", "arbitrary"),
        name="hgrn2_bidir",
    )(p, p, p, p, p, p, lb_f, lb_b, tri, s0)


def _hg_out_body(of_ref, ob_ref, g_ref, gain_ref, y_ref):
    o = of_ref[...] + ob_ref[...]
    on = o * lax.rsqrt(jnp.mean(o * o, axis=-1, keepdims=True) + NORM_EPS) * gain_ref[...]
    g = g_ref[...]
    y_ref[...] = (on * (g * jax.nn.sigmoid(g))).astype(y_ref.dtype)


def hgrn2_output(o_f, o_b, p, gain, *, n_heads, col_g, tm):
    M = o_f.shape[0]
    blk = (tm, HEAD_DIM)
    return pl.pallas_call(
        _hg_out_body,
        grid=(M // tm, n_heads),
        in_specs=[pl.BlockSpec(blk, lambda i, h: (i, h)),
                  pl.BlockSpec(blk, lambda i, h: (i, h)),
                  pl.BlockSpec(blk, lambda i, h: (i, col_g + h)),
                  pl.BlockSpec((None, 1, HEAD_DIM), lambda i, h: (h, 0, 0))],
        out_specs=pl.BlockSpec(blk, lambda i, h: (i, h)),
        out_shape=jax.ShapeDtypeStruct(o_f.shape, BF16),
        compiler_params=_cparams("parallel", "parallel"),
        name="hgrn2_output",
    )(o_f, o_b, p, gain)


def _ffn_body(x_ref, wg_ref, wu_ref, wd_ref, gate_ref, res_gate_ref, o_ref):
    x = x_ref[...]
    u = jnp.dot(x, wg_ref[...], preferred_element_type=F32)
    w = jnp.dot(x, wu_ref[...], preferred_element_type=F32)
    a = (u * jax.nn.sigmoid(u) * w).astype(BF16)
    o_ref[...] = jnp.dot(a, wd_ref[...], preferred_element_type=F32) * gate_ref[...] * res_gate_ref[0]


def expert_ffn(xe, w_gate, w_up, w_down, gate, res_gate, *, tm):
    B, E, cap, D = xe.shape
    F = w_gate.shape[-1]
    per_sample = res_gate.shape[0] > 1
    return pl.pallas_call(
        _ffn_body,
        grid=(E, B, cap // tm),
        in_specs=[pl.BlockSpec((None, None, tm, D), lambda e, b, r: (b, e, r, 0)),
                  pl.BlockSpec((None, D, F), lambda e, b, r: (e, 0, 0)),
                  pl.BlockSpec((None, D, F), lambda e, b, r: (e, 0, 0)),
                  pl.BlockSpec((None, F, D), lambda e, b, r: (e, 0, 0)),
                  pl.BlockSpec((None, None, tm, 1), lambda e, b, r: (b, e, r, 0)),
                  pl.BlockSpec((1, 1, D), lambda e, b, r: (b if per_sample else 0, 0, 0))],
        out_specs=pl.BlockSpec((None, None, tm, D), lambda e, b, r: (b, e, r, 0)),
        out_shape=jax.ShapeDtypeStruct((B, E, cap, D), F32),
        compiler_params=_cparams("parallel", "parallel", "arbitrary"),
        name="expert_ffn",
    )(xe, w_gate, w_up, w_down, gate, res_gate)


def _expert_choice_ffn(x, h, logits, res_gate, w_gate, w_up, w_down):
    B, n, D = x.shape
    cap = EC_CAPACITY * n // N_EXPERTS
    aff = jax.nn.softmax(logits, axis=-1)
    gate, idx = lax.top_k(aff.transpose(0, 2, 1), cap)
    bi = jnp.arange(B)[:, None, None]
    xe = h[bi, idx]
    y = expert_ffn(xe, w_gate, w_up, w_down, gate[..., None], res_gate, tm=min(cap, 512))
    return x.at[bi, idx].add(y)


def kernel(x, c, ctx, c_ctx, mod_down, mod_up, mod_bias, mix_norm, ffn_norm, w_in, w_out,
           conv_w, conv_b, conv_ln_g, conv_ln_b, na_rpb, hg_lb_logits, hg_norm, router,
           w_gate, w_up, w_down, final_norm):
    B, N, D = x.shape
    M = ctx.shape[1]
    depth = w_in.shape[0]
    rows = N // GRID_W
    w_conv = conv_w.shape[-1]
    w_hg = hg_norm.shape[-1]
    n_hg = w_hg // HEAD_DIM
    n_na = na_rpb.shape[1]
    w_na = n_na * HEAD_DIM
    in_conv = 2 * w_conv
    col_na = in_conv // HEAD_DIM
    col_hg = (in_conv + 3 * w_na) // HEAD_DIM
    n_exp = router.shape[-1]

    lb_p = jax.nn.softmax(hg_lb_logits.astype(F32), axis=1)
    lb_all = jnp.cumsum(lb_p, axis=1) - lb_p[:, :1]
    a_lat = jax.nn.silu(c)
    a_ctx = jax.nn.silu(c_ctx)
    cos_tab, sin_tab = _rope_tables(N)
    s_zero = jnp.zeros((B * n_hg, 2, HEAD_DIM, HEAD_DIM), F32)

    xl = x.reshape(B * N, D)
    xc = ctx.reshape(B * M, D)
    for l in range(depth):
        last = l == depth - 1
        hi = lax.Precision.HIGHEST
        mod = jnp.dot(jnp.dot(a_lat, mod_down[l], precision=hi), mod_up[l], precision=hi) + mod_bias[l]
        mod_c = jnp.dot(jnp.dot(a_ctx, mod_down[l], precision=hi), mod_up[l], precision=hi) + mod_bias[l]
        sh1, sc1, g1, sh2, sc2, g2 = [m[:, None, :] for m in jnp.split(mod, N_MOD, axis=-1)]
        csh1, csc1, cg1, csh2, csc2, cg2 = [m[None, None, :] for m in jnp.split(mod_c, N_MOD, axis=-1)]

        wg_l, wu_l, wd_l = w_gate[l].astype(BF16), w_up[l].astype(BF16), w_down[l].astype(BF16)
        g_mix = mix_norm[l][None, :]
        g_ffn = ffn_norm[l][None, :]
        router_l = jnp.pad(router[l], ((0, 0), (0, ROUTER_PAD - n_exp)))
        conv_args = (conv_w[l], conv_b[l][None, :], conv_ln_g[l][None, :], conv_ln_b[l][None, :])

        tn_in = _tile(w_in.shape[2], 512)
        tn_out = _tile(D, 512)
        tm_lat = _tile(N, 1024)
        h1 = norm_mod(xl, g_mix, sc1, sh1, tm=512)
        hc1 = norm_mod(xc, g_mix, csc1, csh1, tm=B * M)
        p = matmul_wcast(h1, w_in[l], tm=tm_lat, tn=tn_in)
        pc = matmul_wcast(hc1, w_in[l], tm=B * M, tn=tn_in)

        ya = conformer_conv(p, *conv_args, batch=B, tt=_tile(N, 512))
        bias = _na_bias(na_rpb[l], rows)
        kr, vb = rope_kv(p, cos_tab, sin_tab, batch=B, n_heads=n_na, col_k=col_na + n_na, tq=_tile(N, 512))
        yb = neighbourhood_attention(p, pc, kr, vb, bias, cos_tab, sin_tab, batch=B, n_heads=n_na,
                                     col_q=col_na, ctx_len=M)
        lb_f = lb_all[0, l].reshape(n_hg, 1, HEAD_DIM)
        lb_b = lb_all[1, l].reshape(n_hg, 1, HEAD_DIM)
        oc_f, oc_b, s_ctx = hgrn2_bidir(pc, lb_f, lb_b, s_zero, batch=B, n_heads=n_hg, col0=col_hg, tb=M)
        o_f, o_b, _ = hgrn2_bidir(p, lb_f, lb_b, s_ctx, batch=B, n_heads=n_hg, col0=col_hg, tb=HG_TB)
        gain = hg_norm[l].reshape(n_hg, 1, HEAD_DIM)
        yc = hgrn2_output(o_f, o_b, p, gain, n_heads=n_hg, col_g=col_hg + 4 * n_hg, tm=512)

        xl = outproj_residual(ya, yb, yc, w_out[l], xl, g1, tm=tm_lat, tn=tn_out)
        h2, logits = norm_mod(xl, g_ffn, sc2, sh2, tm=512, router=router_l)
        xl = _expert_choice_ffn(xl.reshape(B, N, D), h2.reshape(B, N, D),
                                logits[:, :n_exp].reshape(B, N, n_exp), g2, wg_l, wu_l, wd_l).reshape(B * N, D)

        if not last:
            yac = conformer_conv(pc, *conv_args, batch=B, tt=M)
            ybc = context_attention(pc, batch=B, n_heads=n_na, col_q=col_na, ctx_len=M)
            ycc = hgrn2_output(oc_f, oc_b, pc, gain, n_heads=n_hg, col_g=col_hg + 4 * n_hg, tm=B * M)
            xc = outproj_residual(yac, ybc, ycc, w_out[l], xc, cg1, tm=B * M, tn=tn_out)
            hc2, logits_c = norm_mod(xc, g_ffn, csc2, csh2, tm=B * M, router=router_l)
            xc = _expert_choice_ffn(xc.reshape(B, M, D), hc2.reshape(B, M, D),
                                    logits_c[:, :n_exp].reshape(B, M, n_exp), cg2, wg_l, wu_l, wd_l).reshape(B * M, D)

    zero = jnp.zeros((1, 1, D), F32)
    out = norm_mod(xl, final_norm[None, :], zero, zero, tm=512, out_dtype=F32)
    return out.reshape(B, N, D)
```

```python
import functools

import numpy as np
import jax
import jax.numpy as jnp
from jax import lax
from jax.experimental import pallas as pl
from jax.experimental.pallas import tpu as pltpu

GRID_W = 64
HEAD_DIM = 128
CONV_K = 31
NA_ROWS = 8
NA_COLS = 16
ROPE_THETA = 10000.0
F_MIN = 1e-6
N_EXPERTS = 16
EC_CAPACITY = 2
N_MOD = 6
NORM_EPS = 1e-6

BF16 = jnp.bfloat16
F32 = jnp.float32

VMEM_LIMIT_BYTES = 56 * 1024 * 1024
NEG_BIG = -1e30
NA_TILE_ROWS = 4
NA_TQ = NA_TILE_ROWS * GRID_W
SUBLANES = 8
CONV_HALO = 16
CONV_ROWS = 32
ROUTER_PAD = 128
HG_SUB = 16
HG_CHUNKS = (64, 32)
HG_SAFE_DECAY = 80.0
HG_TRI = 128
HG_TB = 512


def _cparams(*sem):
    return pltpu.CompilerParams(dimension_semantics=sem, vmem_limit_bytes=VMEM_LIMIT_BYTES)


def _tile(n, preferred):
    t = preferred
    while n % t:
        t //= 2
    return t


def _norm_mod(x, g, sc, sh):
    r = lax.rsqrt(jnp.mean(x * x, axis=-1, keepdims=True) + NORM_EPS)
    return (x * r * g) * (1.0 + sc) + sh


def _nm_body(x_ref, g_ref, sc_ref, sh_ref, *rest):
    h = _norm_mod(x_ref[...], g_ref[...], sc_ref[0], sh_ref[0])
    if len(rest) == 3:
        r_ref, o_ref, logit_ref = rest
        logit_ref[...] = jnp.dot(h, r_ref[...], preferred_element_type=F32, precision=lax.Precision.HIGHEST)
    else:
        o_ref, = rest
    o_ref[...] = h.astype(o_ref.dtype)


def norm_mod(x, g, sc, sh, *, tm, router=None, out_dtype=BF16):
    M, D = x.shape
    G = sc.shape[0]
    tiles_per_group = M // G // tm
    in_specs = [
        pl.BlockSpec((tm, D), lambda i: (i, 0)),
        pl.BlockSpec((1, D), lambda i: (0, 0)),
        pl.BlockSpec((1, 1, D), lambda i: (i // tiles_per_group, 0, 0)),
        pl.BlockSpec((1, 1, D), lambda i: (i // tiles_per_group, 0, 0)),
    ]
    out_specs = pl.BlockSpec((tm, D), lambda i: (i, 0))
    out_shape = jax.ShapeDtypeStruct((M, D), out_dtype)
    args = (x, g, sc, sh)
    if router is not None:
        E = router.shape[1]
        in_specs.append(pl.BlockSpec((D, E), lambda i: (0, 0)))
        out_specs = [out_specs, pl.BlockSpec((tm, E), lambda i: (i, 0))]
        out_shape = [out_shape, jax.ShapeDtypeStruct((M, E), F32)]
        args += (router,)
    return pl.pallas_call(
        _nm_body,
        grid=(M // tm,),
        in_specs=in_specs,
        out_specs=out_specs,
        out_shape=out_shape,
        compiler_params=_cparams("parallel"),
        name="norm_mod",
    )(*args)


def _mm_body(h_ref, w_ref, o_ref, wb_scr):
    @pl.when(pl.program_id(1) == 0)
    def _():
        wb_scr[...] = w_ref[...].astype(BF16)

    o_ref[...] = jnp.dot(h_ref[...], wb_scr[...], preferred_element_type=F32)


def matmul_wcast(h, w, layer, *, tm, tn):
    M, K = h.shape
    N = w.shape[2]
    return pl.pallas_call(
        _mm_body,
        grid=(N // tn, M // tm),
        in_specs=[pl.BlockSpec((tm, K), lambda j, i: (i, 0)),
                  pl.BlockSpec((None, K, tn), lambda j, i: (layer, 0, j))],
        out_specs=pl.BlockSpec((tm, tn), lambda j, i: (i, j)),
        out_shape=jax.ShapeDtypeStruct((M, N), F32),
        scratch_shapes=[pltpu.VMEM((K, tn), BF16)],
        compiler_params=_cparams("parallel", "arbitrary"),
        name="matmul_wcast",
    )(h, w)


def _outproj_body(ya_ref, yb_ref, yc_ref, w_ref, x_ref, g_ref, o_ref, wb_scr):
    @pl.when(pl.program_id(1) == 0)
    def _():
        wb_scr[...] = w_ref[...].astype(BF16)

    ka, kb = ya_ref.shape[1], yb_ref.shape[1]
    acc = jnp.dot(ya_ref[...], wb_scr[0:ka, :], preferred_element_type=F32)
    acc += jnp.dot(yb_ref[...], wb_scr[ka:ka + kb, :], preferred_element_type=F32)
    acc += jnp.dot(yc_ref[...], wb_scr[ka + kb:, :], preferred_element_type=F32)
    o_ref[...] = x_ref[...] + g_ref[0] * acc


def outproj_residual(ya, yb, yc, w, layer, x, gate, *, tm, tn):
    M, D = x.shape
    G = gate.shape[0]
    tiles_per_group = M // G // tm
    ka, kb, kc = ya.shape[1], yb.shape[1], yc.shape[1]
    K = ka + kb + kc
    return pl.pallas_call(
        _outproj_body,
        grid=(D // tn, M // tm),
        in_specs=[
            pl.BlockSpec((tm, ka), lambda j, i: (i, 0)),
            pl.BlockSpec((tm, kb), lambda j, i: (i, 0)),
            pl.BlockSpec((tm, kc), lambda j, i: (i, 0)),
            pl.BlockSpec((None, K, tn), lambda j, i: (layer, 0, j)),
            pl.BlockSpec((tm, tn), lambda j, i: (i, j)),
            pl.BlockSpec((1, 1, tn), lambda j, i: (i // tiles_per_group, 0, j)),
        ],
        out_specs=pl.BlockSpec((tm, tn), lambda j, i: (i, j)),
        out_shape=jax.ShapeDtypeStruct((M, D), F32),
        scratch_shapes=[pltpu.VMEM((K, tn), BF16)],
        compiler_params=_cparams("parallel", "arbitrary"),
        name="outproj_residual",
    )(ya, yb, yc, w, x, gate)


def _rope_tables(n_tokens):
    half = HEAD_DIM // 2
    n_freq = half // 2
    t = np.arange(n_tokens)
    pos = np.stack([t // GRID_W, t % GRID_W], axis=1).astype(np.float32)
    d = np.arange(HEAD_DIM)
    which = d // half
    sign = np.where((d % half) < n_freq, -1.0, 1.0).astype(np.float32)
    inv_freq = ROPE_THETA ** (-jnp.arange(n_freq, dtype=F32) / n_freq)
    ang = jnp.asarray(pos)[:, which] * inv_freq[d % n_freq][None, :]
    return jnp.cos(ang), jnp.sin(ang) * sign[None, :]


def _na_geometry(rows):
    n_tiles = rows // NA_TILE_ROWS
    kr = min(NA_ROWS, rows)
    a = np.arange(NA_TILE_ROWS)[:, None]
    r = np.arange(3 * NA_TILE_ROWS)[None, :]
    row_sel = np.zeros((3, NA_TILE_ROWS, 3 * NA_TILE_ROWS, 2 * NA_ROWS - 1), np.float32)
    for v, t in enumerate((0, min(1, n_tiles - 1), n_tiles - 1)):
        qrow = NA_TILE_ROWS * t + a
        krow = NA_TILE_ROWS * (t - 1) + r
        rs = np.clip(qrow - kr // 2, 0, rows - kr)
        ok = (krow >= rs) & (krow < rs + kr) & (krow >= 0) & (krow < rows)
        aa, rr = np.nonzero(ok)
        row_sel[v, aa, rr, (krow - qrow + NA_ROWS - 1)[aa, rr]] = 1.0
    c = np.arange(GRID_W)[:, None]
    d = np.arange(GRID_W)[None, :]
    cs = np.clip(c - NA_COLS // 2, 0, GRID_W - NA_COLS)
    cc, dd = np.nonzero((d >= cs) & (d < cs + NA_COLS))
    col_sel = np.zeros((GRID_W, GRID_W, 2 * NA_COLS - 1), np.float32)
    col_sel[cc, dd, (d - c + NA_COLS - 1)[cc, dd]] = 1.0
    return row_sel, col_sel


def _na_bias(rpb, rows):
    row_sel, col_sel = _na_geometry(rows)
    bias = jnp.einsum('vari,hij,cdj->vhacrd', jnp.asarray(row_sel), rpb, jnp.asarray(col_sel),
                      precision=lax.Precision.HIGHEST)
    valid = ((row_sel.sum(-1) > 0)[:, None, :, None, :, None]
             & (col_sel.sum(-1) > 0)[None, None, None, :, None, :])
    bias = jnp.where(jnp.asarray(valid), bias, NEG_BIG)
    return bias.reshape(3, rpb.shape[0], NA_TQ, 3 * NA_TQ)


def _rope(x, cos, sin, first):
    partner = jnp.where(first, pltpu.roll(x, 3 * HEAD_DIM // 4, 1), pltpu.roll(x, HEAD_DIM // 4, 1))
    return x * cos + partner * sin


def _rope_kv_body(k_ref, v_ref, cos_ref, sin_ref, kr_ref, vb_ref):
    lane = lax.broadcasted_iota(jnp.int32, cos_ref.shape, 1)
    first = (lane % (HEAD_DIM // 2)) < (HEAD_DIM // 4)
    cos, sin = cos_ref[...], sin_ref[...]
    for h in range(k_ref.shape[1] // HEAD_DIM):
        cols = slice(h * HEAD_DIM, (h + 1) * HEAD_DIM)
        kr_ref[:, cols] = _rope(k_ref[:, cols], cos, sin, first).astype(BF16)
    vb_ref[...] = v_ref[...].astype(BF16)


def _head_group(n_heads, *cols):
    g = 4
    while n_heads % g or any(c % g for c in cols):
        g //= 2
    return g


def rope_kv(p, cos_tab, sin_tab, *, batch, n_heads, col_k, tq):
    tiles = p.shape[0] // batch // tq
    g = _head_group(n_heads, col_k, col_k + n_heads)
    blk = (tq, g * HEAD_DIM)
    tab = (tq, HEAD_DIM)
    shape = jax.ShapeDtypeStruct((p.shape[0], n_heads * HEAD_DIM), BF16)
    return pl.pallas_call(
        _rope_kv_body,
        grid=(p.shape[0] // tq, n_heads // g),
        in_specs=[pl.BlockSpec(blk, lambda i, h: (i, col_k // g + h)),
                  pl.BlockSpec(blk, lambda i, h: (i, (col_k + n_heads) // g + h)),
                  pl.BlockSpec(tab, lambda i, h: (i % tiles, 0)),
                  pl.BlockSpec(tab, lambda i, h: (i % tiles, 0))],
        out_specs=[pl.BlockSpec(blk, lambda i, h: (i, h)), pl.BlockSpec(blk, lambda i, h: (i, h))],
        out_shape=[shape, shape],
        compiler_params=_cparams("parallel", "arbitrary"),
        name="rope_kv",
    )(p, p, cos_tab, sin_tab)


def _na_body(q_ref, kp_ref, kc_ref, kn_ref, vp_ref, vc_ref, vn_ref, kx_ref, vx_ref,
             cos_ref, sin_ref, bias_ref, o_ref):
    scale = HEAD_DIM ** -0.5
    lane = lax.broadcasted_iota(jnp.int32, (NA_TQ, HEAD_DIM), 1)
    first = (lane % (HEAD_DIM // 2)) < (HEAD_DIM // 4)
    q = q_ref[...]
    qr = _rope(q, cos_ref[...], sin_ref[...], first).astype(BF16)
    kr = jnp.concatenate([kp_ref[...], kc_ref[...], kn_ref[...]], axis=0)
    s_lat = lax.dot_general(qr, kr, _NT, preferred_element_type=F32) * scale + bias_ref[0, 0]
    s_ctx = lax.dot_general(q.astype(BF16), kx_ref[...].astype(BF16), _NT,
                            preferred_element_type=F32) * scale
    m = jnp.maximum(jnp.max(s_lat, axis=-1, keepdims=True), jnp.max(s_ctx, axis=-1, keepdims=True))
    p_lat = jnp.exp(s_lat - m)
    p_ctx = jnp.exp(s_ctx - m)
    denom = jnp.sum(p_lat, axis=-1, keepdims=True) + jnp.sum(p_ctx, axis=-1, keepdims=True)
    v_all = jnp.concatenate([vp_ref[...], vc_ref[...], vn_ref[...]], axis=0)
    o = (jnp.dot(p_lat.astype(BF16), v_all, preferred_element_type=F32)
         + jnp.dot(p_ctx.astype(BF16), vx_ref[...].astype(BF16), preferred_element_type=F32))
    o_ref[...] = (o / denom).astype(o_ref.dtype)


def neighbourhood_attention(p, pc, kr, vb, bias, cos_tab, sin_tab, *, batch, n_heads, col_q, ctx_len):
    n_tok = p.shape[0] // batch
    n_tiles = n_tok // NA_TQ
    col_k, col_v = col_q + n_heads, col_q + 2 * n_heads
    last = n_tiles - 1

    def tile(h, b, t):
        return b * n_tiles + t

    def prev(h, b, t):
        return b * n_tiles + jnp.maximum(t - 1, 0)

    def nxt(h, b, t):
        return b * n_tiles + jnp.minimum(t + 1, last)

    def variant(h, b, t):
        return jnp.where(t == 0, 0, jnp.where(t == last, 2, 1))

    blk = (NA_TQ, HEAD_DIM)
    ctx_blk = (ctx_len, HEAD_DIM)
    in_specs = [
        pl.BlockSpec(blk, lambda h, b, t: (tile(h, b, t), col_q + h)),
        pl.BlockSpec(blk, lambda h, b, t: (prev(h, b, t), h)),
        pl.BlockSpec(blk, lambda h, b, t: (tile(h, b, t), h)),
        pl.BlockSpec(blk, lambda h, b, t: (nxt(h, b, t), h)),
        pl.BlockSpec(blk, lambda h, b, t: (prev(h, b, t), h)),
        pl.BlockSpec(blk, lambda h, b, t: (tile(h, b, t), h)),
        pl.BlockSpec(blk, lambda h, b, t: (nxt(h, b, t), h)),
        pl.BlockSpec(ctx_blk, lambda h, b, t: (b, col_k + h)),
        pl.BlockSpec(ctx_blk, lambda h, b, t: (b, col_v + h)),
        pl.BlockSpec(blk, lambda h, b, t: (t, 0)),
        pl.BlockSpec(blk, lambda h, b, t: (t, 0)),
        pl.BlockSpec((1, 1, NA_TQ, 3 * NA_TQ), lambda h, b, t: (variant(h, b, t), h, 0, 0)),
    ]
    return pl.pallas_call(
        _na_body,
        grid=(n_heads, batch, n_tiles),
        in_specs=in_specs,
        out_specs=pl.BlockSpec(blk, lambda h, b, t: (tile(h, b, t), h)),
        out_shape=jax.ShapeDtypeStruct((p.shape[0], n_heads * HEAD_DIM), BF16),
        compiler_params=_cparams("parallel", "parallel", "arbitrary"),
        name="neighbourhood_attention",
    )(p, kr, kr, kr, vb, vb, vb, pc, pc, cos_tab, sin_tab, bias)


def _ctx_attn_body(q_ref, k_ref, v_ref, o_ref):
    nt = (((1,), (1,)), ((), ()))
    s = lax.dot_general(q_ref[...].astype(BF16), k_ref[...].astype(BF16), nt,
                        preferred_element_type=F32) * HEAD_DIM ** -0.5
    p = jnp.exp(s - jnp.max(s, axis=-1, keepdims=True))
    denom = jnp.sum(p, axis=-1, keepdims=True)
    o = jnp.dot(p.astype(BF16), v_ref[...].astype(BF16), preferred_element_type=F32)
    o_ref[...] = (o / denom).astype(o_ref.dtype)


def context_attention(pc, *, batch, n_heads, col_q, ctx_len):
    col_k, col_v = col_q + n_heads, col_q + 2 * n_heads
    blk = (ctx_len, HEAD_DIM)
    return pl.pallas_call(
        _ctx_attn_body,
        grid=(batch, n_heads),
        in_specs=[pl.BlockSpec(blk, lambda b, h: (b, col_q + h)),
                  pl.BlockSpec(blk, lambda b, h: (b, col_k + h)),
                  pl.BlockSpec(blk, lambda b, h: (b, col_v + h))],
        out_specs=pl.BlockSpec(blk, lambda b, h: (b, h)),
        out_shape=jax.ShapeDtypeStruct((pc.shape[0], n_heads * HEAD_DIM), BF16),
        compiler_params=_cparams("parallel", "parallel"),
        name="context_attention",
    )(pc, pc, pc)


def _conv_body(val_ref, gate_ref, pval_ref, pgate_ref, nval_ref, ngate_ref, w_ref, b_ref, lg_ref, lb_ref,
               o_ref, u_scr, *, tt):
    t = pl.program_id(1)

    def glu(v_ref, g_ref):
        return v_ref[...] * jax.nn.sigmoid(g_ref[...])

    span = tt + 2 * CONV_HALO
    u_scr[0, 0:CONV_HALO, :] = jnp.where(t > 0, glu(pval_ref, pgate_ref), 0.0)
    u_scr[0, CONV_HALO:CONV_HALO + tt, :] = glu(val_ref, gate_ref)
    u_scr[0, CONV_HALO + tt:span, :] = jnp.where(t < pl.num_programs(1) - 1, glu(nval_ref, ngate_ref), 0.0)
    u_scr[0, span:, :] = jnp.zeros((SUBLANES, u_scr.shape[2]), F32)
    for a in range(1, SUBLANES):
        u_scr[a, 0:span, :] = u_scr[0, a:a + span, :]
    off = CONV_HALO - CONV_K // 2

    def tap(r, k):
        start = r + off + k
        a = start % SUBLANES
        return w_ref[k:k + 1, :] * u_scr[a, start - a:start - a + CONV_ROWS, :]

    for r in range(0, tt, CONV_ROWS):
        acc = tap(r, 0)
        for k in range(1, CONV_K):
            acc = acc + tap(r, k)
        acc = acc + b_ref[...]
        mu = jnp.mean(acc, axis=-1, keepdims=True)
        cen = acc - mu
        var = jnp.mean(cen * cen, axis=-1, keepdims=True)
        un = cen * lax.rsqrt(var + NORM_EPS) * lg_ref[...] + lb_ref[...]
        o_ref[r:r + CONV_ROWS, :] = (un * jax.nn.sigmoid(un)).astype(o_ref.dtype)


def conformer_conv(p, w_dw, b_dw, ln_g, ln_b, *, batch, tt):
    C = w_dw.shape[1]
    T = p.shape[0] // batch
    nt = T // tt
    hb = tt // CONV_HALO
    seq_hb = T // CONV_HALO

    def prev_blk(b, t):
        return jnp.maximum(b * seq_hb + t * hb - 1, 0)

    def next_blk(b, t):
        return jnp.minimum(b * seq_hb + (t + 1) * hb, batch * seq_hb - 1)

    main = (tt, C)
    halo = (CONV_HALO, C)
    vec = pl.BlockSpec((1, C), lambda b, t: (0, 0))
    return pl.pallas_call(
        functools.partial(_conv_body, tt=tt),
        grid=(batch, nt),
        in_specs=[pl.BlockSpec(main, lambda b, t: (b * nt + t, 0)),
                  pl.BlockSpec(main, lambda b, t: (b * nt + t, 1)),
                  pl.BlockSpec(halo, lambda b, t: (prev_blk(b, t), 0)),
                  pl.BlockSpec(halo, lambda b, t: (prev_blk(b, t), 1)),
                  pl.BlockSpec(halo, lambda b, t: (next_blk(b, t), 0)),
                  pl.BlockSpec(halo, lambda b, t: (next_blk(b, t), 1)),
                  pl.BlockSpec((CONV_K, C), lambda b, t: (0, 0)), vec, vec, vec],
        out_specs=pl.BlockSpec(main, lambda b, t: (b * nt + t, 0)),
        out_shape=jax.ShapeDtypeStruct((p.shape[0], C), BF16),
        scratch_shapes=[pltpu.VMEM((SUBLANES, tt + 2 * CONV_HALO + SUBLANES, C), F32)],
        compiler_params=_cparams("parallel", "parallel"),
        name="conformer_conv",
    )(p, p, p, p, p, p, w_dw, b_dw, ln_g, ln_b)


_NT = (((1,), (1,)), ((), ()))
_TN = (((0,), (0,)), ((), ()))


def _hg_gates(zq, zf, lb):
    q = zq * jax.nn.sigmoid(zq) * HEAD_DIM ** -0.5
    f = lb + (1.0 - lb) * jax.nn.sigmoid(zf)
    log_f = jnp.log(jnp.maximum(f, F_MIN))
    k = (1.0 - lb) * jax.nn.sigmoid(-zf)
    return q, k, log_f


def _block_cumsum(tri, g):
    out = []
    for r in range(g.shape[0] // tri.shape[0]):
        x = g[r * tri.shape[0]:(r + 1) * tri.shape[0], :]
        hi = x.astype(BF16)
        rest = x - hi.astype(F32)
        mid = rest.astype(BF16)
        lo = (rest - mid.astype(F32)).astype(BF16)
        out.append(jnp.dot(tri, hi, preferred_element_type=F32)
                   + jnp.dot(tri, mid, preferred_element_type=F32)
                   + jnp.dot(tri, lo, preferred_element_type=F32))
    return jnp.concatenate(out, axis=0)


def _hg_chunk(r, d, st, qe_s, k_s, b_s, v_ref, o_ref, reverse, L):
    qe = qe_s[d, r:r + L, :]
    ke = k_s[d, r:r + L, :]
    vI = v_ref[r:r + L, :].astype(BF16)
    att = lax.dot_general(qe, ke.astype(BF16), _NT, preferred_element_type=F32)
    ti = lax.broadcasted_iota(jnp.int32, (L, L), 0)
    si = lax.broadcasted_iota(jnp.int32, (L, L), 1)
    att = jnp.where((si >= ti) if reverse else (si <= ti), att, 0.0)
    o_ref[r:r + L, :] = (jnp.dot(att.astype(BF16), vI, preferred_element_type=F32)
                         + lax.dot_general(qe, st.astype(BF16), _NT, preferred_element_type=F32))
    end = r if reverse else r + L - 1
    e_end = jnp.exp(b_s[d, end:end + 1, :])
    upd = lax.dot_general(vI, (ke * e_end).astype(BF16), _TN, preferred_element_type=F32)
    return st * e_end + upd


def _hg_sub_block(r, d, q_s, k_s, b_s, v_ref, o_ref, st_s, reverse):
    nt, tn = _NT, _TN
    bI = b_s[d, pl.ds(r, HG_SUB), :]
    qI = q_s[d, pl.ds(r, HG_SUB), :]
    kI = k_s[d, pl.ds(r, HG_SUB), :]
    vI = v_ref[pl.ds(r, HG_SUB), :]
    st = st_s[d]
    o_inter = lax.dot_general((qI * jnp.exp(bI)).astype(BF16), st.astype(BF16), nt,
                              preferred_element_type=F32)
    sub = lax.broadcasted_iota(jnp.int32, (HG_SUB, HEAD_DIM), 0)
    slabs = []
    for t in range(HG_SUB):
        keep = (sub >= t) if reverse else (sub <= t)
        decay = jnp.where(keep, jnp.exp(bI[t:t + 1, :] - bI), 0.0)
        slabs.append(decay * (qI[t:t + 1, :] * kI))
    x3 = jnp.concatenate(slabs, axis=0).astype(BF16)
    att = jnp.dot(x3, jnp.ones((HEAD_DIM, HEAD_DIM), BF16), preferred_element_type=F32)
    z = att * jnp.concatenate([vI] * HG_SUB, axis=0)
    o_intra = jnp.sum(z.reshape(HG_SUB, HG_SUB, HEAD_DIM), axis=1)
    o_ref[pl.ds(r, HG_SUB), :] = o_inter + o_intra
    b_end = bI[0:1, :] if reverse else bI[HG_SUB - 1:HG_SUB, :]
    k_dec = (kI * jnp.exp(b_end - bI)).astype(BF16)
    upd = lax.dot_general(vI.astype(BF16), k_dec, tn, preferred_element_type=F32)
    st_s[d] = st * jnp.exp(b_end) + upd


def _hg_body(zqf_ref, zvf_ref, zff_ref, zqb_ref, zvb_ref, zfb_ref, lbf_ref, lbb_ref,
             tri_ref, s0_ref, of_ref, ob_ref, sout_ref,
             q_s, k_s, g_s, b_s, qe_s, st_s, *, tb):
    c = pl.program_id(1)

    @pl.when(c == 0)
    def _():
        st_s[...] = s0_ref[...]

    for d, (zq_ref, zf_ref, lb_ref) in enumerate(((zqf_ref, zff_ref, lbf_ref), (zqb_ref, zfb_ref, lbb_ref))):
        q_s[d], k_s[d], g_s[d] = _hg_gates(zq_ref[...], zf_ref[...], lb_ref[...])

    def cumulate(level):
        low = None
        for d in range(2):
            b = _block_cumsum(tri_ref[2 * level + d], g_s[d])
            b_s[d] = b
            low = jnp.min(b) if low is None else jnp.minimum(low, jnp.min(b))
        return low

    def factored(L):
        for d in range(2):
            b = b_s[d]
            qe_s[d] = (q_s[d] * jnp.exp(b)).astype(BF16)
            k_s[d] = k_s[d] * jnp.exp(-b)
        n = tb // L
        st_f, st_b = st_s[0], st_s[1]
        for i in range(n):
            st_f = _hg_chunk(i * L, 0, st_f, qe_s, k_s, b_s, zvf_ref, of_ref, False, L)
            st_b = _hg_chunk((n - 1 - i) * L, 1, st_b, qe_s, k_s, b_s, zvb_ref, ob_ref, True, L)
        st_s[0] = st_f
        st_s[1] = st_b

    def dispatch(level):
        if level == len(HG_CHUNKS):
            exact()
            return
        safe = cumulate(level) >= -HG_SAFE_DECAY
        pl.when(safe)(lambda: factored(HG_CHUNKS[level]))
        pl.when(jnp.logical_not(safe))(lambda: dispatch(level + 1))

    def exact():
        cumulate(len(HG_CHUNKS))
        n = tb // HG_SUB

        def step(i, carry):
            rf = pl.multiple_of(i * HG_SUB, HG_SUB)
            _hg_sub_block(rf, 0, q_s, k_s, b_s, zvf_ref, of_ref, st_s, False)
            rb = pl.multiple_of((n - 1 - i) * HG_SUB, HG_SUB)
            _hg_sub_block(rb, 1, q_s, k_s, b_s, zvb_ref, ob_ref, st_s, True)
            return carry

        lax.fori_loop(0, n, step, 0)

    dispatch(0)

    @pl.when(c == pl.num_programs(1) - 1)
    def _():
        sout_ref[...] = st_s[...]


def hgrn2_bidir(p, lb_f, lb_b, s0, *, batch, n_heads, col0, tb):
    n_tok = p.shape[0] // batch
    nb = n_tok // tb
    H = n_heads
    tri = []
    for size in HG_CHUNKS + (HG_SUB,):
        eye = np.kron(np.eye(HG_TRI // size), np.ones((size, size)))
        tri += [np.tril(eye), np.triu(eye)]
    tri = jnp.asarray(np.stack(tri), BF16)

    def fwd(col):
        return pl.BlockSpec((tb, HEAD_DIM), lambda bh, c: ((bh // H) * nb + c, col + bh % H))

    def bwd(col):
        return pl.BlockSpec((tb, HEAD_DIM), lambda bh, c: ((bh // H) * nb + nb - 1 - c, col + bh % H))

    def out(rev):
        if rev:
            return pl.BlockSpec((tb, HEAD_DIM), lambda bh, c: ((bh // H) * nb + nb - 1 - c, bh % H))
        return pl.BlockSpec((tb, HEAD_DIM), lambda bh, c: ((bh // H) * nb + c, bh % H))

    lb_spec = pl.BlockSpec((None, 1, HEAD_DIM), lambda bh, c: (bh % H, 0, 0))
    tri_spec = pl.BlockSpec(tri.shape, lambda bh, c: (0, 0, 0))
    st_spec = pl.BlockSpec((None, 2, HEAD_DIM, HEAD_DIM), lambda bh, c: (bh, 0, 0, 0))
    o_shape = jax.ShapeDtypeStruct((p.shape[0], H * HEAD_DIM), F32)
    dir_buf = pltpu.VMEM((2, tb, HEAD_DIM), F32)
    return pl.pallas_call(
        functools.partial(_hg_body, tb=tb),
        grid=(batch * H, nb),
        in_specs=[fwd(col0), fwd(col0 + H), fwd(col0 + 2 * H),
                  bwd(col0), bwd(col0 + H), bwd(col0 + 3 * H),
                  lb_spec, lb_spec, tri_spec, st_spec],
        out_specs=[out(False), out(True), st_spec],
        out_shape=[o_shape, o_shape, jax.ShapeDtypeStruct(s0.shape, F32)],
        scratch_shapes=[dir_buf, dir_buf, dir_buf, dir_buf, pltpu.VMEM((2, tb, HEAD_DIM), BF16),
                        pltpu.VMEM((2, HEAD_DIM, HEAD_DIM), F32)],
        compiler_params=_cparams("parallel", "arbitrary"),
        name="hgrn2_bidir",
    )(p, p, p, p, p, p, lb_f, lb_b, tri, s0)


def _hg_out_body(of_ref, ob_ref, g_ref, gain_ref, y_ref):
    for h in range(of_ref.shape[1] // HEAD_DIM):
        cols = slice(h * HEAD_DIM, (h + 1) * HEAD_DIM)
        o = of_ref[:, cols] + ob_ref[:, cols]
        on = o * lax.rsqrt(jnp.mean(o * o, axis=-1, keepdims=True) + NORM_EPS) * gain_ref[:, cols]
        g = g_ref[:, cols]
        y_ref[:, cols] = (on * (g * jax.nn.sigmoid(g))).astype(y_ref.dtype)


def hgrn2_output(o_f, o_b, p, gain, *, n_heads, col_g, tm):
    M = o_f.shape[0]
    grp = _head_group(n_heads, col_g)
    blk = (tm, grp * HEAD_DIM)
    return pl.pallas_call(
        _hg_out_body,
        grid=(M // tm, n_heads // grp),
        in_specs=[pl.BlockSpec(blk, lambda i, h: (i, h)),
                  pl.BlockSpec(blk, lambda i, h: (i, h)),
                  pl.BlockSpec(blk, lambda i, h: (i, col_g // grp + h)),
                  pl.BlockSpec((1, grp * HEAD_DIM), lambda i, h: (0, h))],
        out_specs=pl.BlockSpec(blk, lambda i, h: (i, h)),
        out_shape=jax.ShapeDtypeStruct(o_f.shape, BF16),
        compiler_params=_cparams("parallel", "parallel"),
        name="hgrn2_output",
    )(o_f, o_b, p, gain)


def _ffn_body(x_ref, wg_ref, wu_ref, wd_ref, gate_ref, res_gate_ref, o_ref):
    x = x_ref[...]
    u = jnp.dot(x, wg_ref[...], preferred_element_type=F32)
    w = jnp.dot(x, wu_ref[...], preferred_element_type=F32)
    a = (u * jax.nn.sigmoid(u) * w).astype(BF16)
    o_ref[...] = jnp.dot(a, wd_ref[...], preferred_element_type=F32) * gate_ref[...] * res_gate_ref[0]


def expert_ffn(xe, w_gate, w_up, w_down, layer, gate, res_gate, *, tm):
    B, E, cap, D = xe.shape
    F = w_gate.shape[-1]
    per_sample = res_gate.shape[0] > 1
    return pl.pallas_call(
        _ffn_body,
        grid=(E, B, cap // tm),
        in_specs=[pl.BlockSpec((None, None, tm, D), lambda e, b, r: (b, e, r, 0)),
                  pl.BlockSpec((None, None, D, F), lambda e, b, r: (layer, e, 0, 0)),
                  pl.BlockSpec((None, None, D, F), lambda e, b, r: (layer, e, 0, 0)),
                  pl.BlockSpec((None, None, F, D), lambda e, b, r: (layer, e, 0, 0)),
                  pl.BlockSpec((None, None, tm, 1), lambda e, b, r: (b, e, r, 0)),
                  pl.BlockSpec((1, 1, D), lambda e, b, r: (b if per_sample else 0, 0, 0))],
        out_specs=pl.BlockSpec((None, None, tm, D), lambda e, b, r: (b, e, r, 0)),
        out_shape=jax.ShapeDtypeStruct((B, E, cap, D), F32),
        compiler_params=_cparams("parallel", "parallel", "arbitrary"),
        name="expert_ffn",
    )(xe, w_gate, w_up, w_down, gate, res_gate)


def _expert_choice_ffn(x, h, logits, res_gate, w_gate, w_up, w_down, layer):
    B, n, D = x.shape
    cap = EC_CAPACITY * n // N_EXPERTS
    aff = jax.nn.softmax(logits, axis=-1)
    gate, idx = lax.top_k(aff.transpose(0, 2, 1), cap)
    bi = jnp.arange(B)[:, None, None]
    xe = h[bi, idx]
    y = expert_ffn(xe, w_gate, w_up, w_down, layer, gate[..., None], res_gate, tm=min(cap, 512))
    return x.at[bi, idx].add(y)


def kernel(x, c, ctx, c_ctx, mod_down, mod_up, mod_bias, mix_norm, ffn_norm, w_in, w_out,
           conv_w, conv_b, conv_ln_g, conv_ln_b, na_rpb, hg_lb_logits, hg_norm, router,
           w_gate, w_up, w_down, final_norm):
    B, N, D = x.shape
    M = ctx.shape[1]
    depth = w_in.shape[0]
    rows = N // GRID_W
    w_conv = conv_w.shape[-1]
    w_hg = hg_norm.shape[-1]
    n_hg = w_hg // HEAD_DIM
    n_na = na_rpb.shape[1]
    w_na = n_na * HEAD_DIM
    in_conv = 2 * w_conv
    col_na = in_conv // HEAD_DIM
    col_hg = (in_conv + 3 * w_na) // HEAD_DIM
    n_exp = router.shape[-1]

    lb_p = jax.nn.softmax(hg_lb_logits.astype(F32), axis=1)
    lb_all = jnp.cumsum(lb_p, axis=1) - lb_p[:, :1]
    a_lat = jax.nn.silu(c)
    a_ctx = jax.nn.silu(c_ctx)
    cos_tab, sin_tab = _rope_tables(N)
    s_zero = jnp.zeros((B * n_hg, 2, HEAD_DIM, HEAD_DIM), F32)
    experts = (w_gate.astype(BF16), w_up.astype(BF16), w_down.astype(BF16))

    xl = x.reshape(B * N, D)
    xc = ctx.reshape(B * M, D)
    for l in range(depth):
        last = l == depth - 1
        hi = lax.Precision.HIGHEST
        mod = jnp.dot(jnp.dot(a_lat, mod_down[l], precision=hi), mod_up[l], precision=hi) + mod_bias[l]
        mod_c = jnp.dot(jnp.dot(a_ctx, mod_down[l], precision=hi), mod_up[l], precision=hi) + mod_bias[l]
        sh1, sc1, g1, sh2, sc2, g2 = [m[:, None, :] for m in jnp.split(mod, N_MOD, axis=-1)]
        csh1, csc1, cg1, csh2, csc2, cg2 = [m[None, None, :] for m in jnp.split(mod_c, N_MOD, axis=-1)]

        g_mix = mix_norm[l][None, :]
        g_ffn = ffn_norm[l][None, :]
        router_l = jnp.pad(router[l], ((0, 0), (0, ROUTER_PAD - n_exp)))
        conv_args = (conv_w[l], conv_b[l][None, :], conv_ln_g[l][None, :], conv_ln_b[l][None, :])

        tn_in = _tile(w_in.shape[2], 512)
        tn_out = _tile(D, 512)
        tm_lat = _tile(N, 1024)
        h1 = norm_mod(xl, g_mix, sc1, sh1, tm=512)
        hc1 = norm_mod(xc, g_mix, csc1, csh1, tm=B * M)
        p = matmul_wcast(h1, w_in, l, tm=tm_lat, tn=tn_in)
        pc = matmul_wcast(hc1, w_in, l, tm=B * M, tn=tn_in)

        ya = conformer_conv(p, *conv_args, batch=B, tt=_tile(N, 512))
        bias = _na_bias(na_rpb[l], rows)
        kr, vb = rope_kv(p, cos_tab, sin_tab, batch=B, n_heads=n_na, col_k=col_na + n_na, tq=_tile(N, 512))
        yb = neighbourhood_attention(p, pc, kr, vb, bias, cos_tab, sin_tab, batch=B, n_heads=n_na,
                                     col_q=col_na, ctx_len=M)
        lb_f = lb_all[0, l].reshape(n_hg, 1, HEAD_DIM)
        lb_b = lb_all[1, l].reshape(n_hg, 1, HEAD_DIM)
        oc_f, oc_b, s_ctx = hgrn2_bidir(pc, lb_f, lb_b, s_zero, batch=B, n_heads=n_hg, col0=col_hg, tb=M)
        o_f, o_b, _ = hgrn2_bidir(p, lb_f, lb_b, s_ctx, batch=B, n_heads=n_hg, col0=col_hg, tb=HG_TB)
        gain = hg_norm[l][None, :]
        yc = hgrn2_output(o_f, o_b, p, gain, n_heads=n_hg, col_g=col_hg + 4 * n_hg, tm=512)

        xl = outproj_residual(ya, yb, yc, w_out, l, xl, g1, tm=tm_lat, tn=tn_out)
        h2, logits = norm_mod(xl, g_ffn, sc2, sh2, tm=512, router=router_l)
        xl = _expert_choice_ffn(xl.reshape(B, N, D), h2.reshape(B, N, D),
                                logits[:, :n_exp].reshape(B, N, n_exp), g2, *experts, l).reshape(B * N, D)

        if not last:
            yac = conformer_conv(pc, *conv_args, batch=B, tt=M)
            ybc = context_attention(pc, batch=B, n_heads=n_na, col_q=col_na, ctx_len=M)
            ycc = hgrn2_output(oc_f, oc_b, pc, gain, n_heads=n_hg, col_g=col_hg + 4 * n_hg, tm=B * M)
            xc = outproj_residual(yac, ybc, ycc, w_out, l, xc, cg1, tm=B * M, tn=tn_out)
            hc2, logits_c = norm_mod(xc, g_ffn, csc2, csh2, tm=B * M, router=router_l)
            xc = _expert_choice_ffn(xc.reshape(B, M, D), hc2.reshape(B, M, D),
                                    logits_c[:, :n_exp].reshape(B, M, n_exp), cg2, *experts, l).reshape(B * M, D)

    zero = jnp.zeros((1, 1, D), F32)
    out = norm_mod(xl, final_norm[None, :], zero, zero, tm=512, out_dtype=F32)
    return out.reshape(B, N, D)
```

```python
import functools

import numpy as np
import jax
import jax.numpy as jnp
from jax import lax
from jax.experimental import pallas as pl
from jax.experimental.pallas import tpu as pltpu

GRID_W = 64
HEAD_DIM = 128
CONV_K = 31
NA_ROWS = 8
NA_COLS = 16
ROPE_THETA = 10000.0
F_MIN = 1e-6
N_EXPERTS = 16
EC_CAPACITY = 2
N_MOD = 6
NORM_EPS = 1e-6

BF16 = jnp.bfloat16
F32 = jnp.float32

VMEM_LIMIT_BYTES = 56 * 1024 * 1024
NEG_BIG = -1e30
NA_TILE_ROWS = 4
NA_TQ = NA_TILE_ROWS * GRID_W
SUBLANES = 8
CONV_HALO = 16
CONV_ROWS = 32
ROUTER_PAD = 128
HG_SUB = 16
HG_CHUNKS = (128, 64)
HG_SAFE_DECAY = 80.0
HG_TRI = 128
HG_TB = 512


def _cparams(*sem):
    return pltpu.CompilerParams(dimension_semantics=sem, vmem_limit_bytes=VMEM_LIMIT_BYTES)


def _tile(n, preferred):
    t = preferred
    while n % t:
        t //= 2
    return t


def _norm_mod(x, g, sc, sh):
    r = lax.rsqrt(jnp.mean(x * x, axis=-1, keepdims=True) + NORM_EPS)
    return (x * r * g) * (1.0 + sc) + sh


def _nm_body(x_ref, g_ref, sc_ref, sh_ref, *rest):
    h = _norm_mod(x_ref[...], g_ref[...], sc_ref[0], sh_ref[0])
    if len(rest) == 3:
        r_ref, o_ref, logit_ref = rest
        logit_ref[...] = jnp.dot(h, r_ref[...], preferred_element_type=F32, precision=lax.Precision.HIGHEST)
    else:
        o_ref, = rest
    o_ref[...] = h.astype(o_ref.dtype)


def norm_mod(x, g, sc, sh, *, tm, router=None, out_dtype=BF16):
    M, D = x.shape
    G = sc.shape[0]
    tiles_per_group = M // G // tm
    in_specs = [
        pl.BlockSpec((tm, D), lambda i: (i, 0)),
        pl.BlockSpec((1, D), lambda i: (0, 0)),
        pl.BlockSpec((1, 1, D), lambda i: (i // tiles_per_group, 0, 0)),
        pl.BlockSpec((1, 1, D), lambda i: (i // tiles_per_group, 0, 0)),
    ]
    out_specs = pl.BlockSpec((tm, D), lambda i: (i, 0))
    out_shape = jax.ShapeDtypeStruct((M, D), out_dtype)
    args = (x, g, sc, sh)
    if router is not None:
        E = router.shape[1]
        in_specs.append(pl.BlockSpec((D, E), lambda i: (0, 0)))
        out_specs = [out_specs, pl.BlockSpec((tm, E), lambda i: (i, 0))]
        out_shape = [out_shape, jax.ShapeDtypeStruct((M, E), F32)]
        args += (router,)
    return pl.pallas_call(
        _nm_body,
        grid=(M // tm,),
        in_specs=in_specs,
        out_specs=out_specs,
        out_shape=out_shape,
        compiler_params=_cparams("parallel"),
        name="norm_mod",
    )(*args)


def _mm_body(h_ref, w_ref, o_ref, wb_scr):
    @pl.when(pl.program_id(1) == 0)
    def _():
        wb_scr[...] = w_ref[...].astype(BF16)

    o_ref[...] = jnp.dot(h_ref[...], wb_scr[...], preferred_element_type=F32)


def matmul_wcast(h, w, layer, *, tm, tn):
    M, K = h.shape
    N = w.shape[2]
    return pl.pallas_call(
        _mm_body,
        grid=(N // tn, M // tm),
        in_specs=[pl.BlockSpec((tm, K), lambda j, i: (i, 0)),
                  pl.BlockSpec((None, K, tn), lambda j, i: (layer, 0, j))],
        out_specs=pl.BlockSpec((tm, tn), lambda j, i: (i, j)),
        out_shape=jax.ShapeDtypeStruct((M, N), F32),
        scratch_shapes=[pltpu.VMEM((K, tn), BF16)],
        compiler_params=_cparams("parallel", "arbitrary"),
        name="matmul_wcast",
    )(h, w)


def _outproj_body(ya_ref, yb_ref, yc_ref, w_ref, x_ref, g_ref, o_ref, wb_scr):
    @pl.when(pl.program_id(1) == 0)
    def _():
        wb_scr[...] = w_ref[...].astype(BF16)

    ka, kb = ya_ref.shape[1], yb_ref.shape[1]
    acc = jnp.dot(ya_ref[...], wb_scr[0:ka, :], preferred_element_type=F32)
    acc += jnp.dot(yb_ref[...], wb_scr[ka:ka + kb, :], preferred_element_type=F32)
    acc += jnp.dot(yc_ref[...], wb_scr[ka + kb:, :], preferred_element_type=F32)
    o_ref[...] = x_ref[...] + g_ref[0] * acc


def outproj_residual(ya, yb, yc, w, layer, x, gate, *, tm, tn):
    M, D = x.shape
    G = gate.shape[0]
    tiles_per_group = M // G // tm
    ka, kb, kc = ya.shape[1], yb.shape[1], yc.shape[1]
    K = ka + kb + kc
    return pl.pallas_call(
        _outproj_body,
        grid=(D // tn, M // tm),
        in_specs=[
            pl.BlockSpec((tm, ka), lambda j, i: (i, 0)),
            pl.BlockSpec((tm, kb), lambda j, i: (i, 0)),
            pl.BlockSpec((tm, kc), lambda j, i: (i, 0)),
            pl.BlockSpec((None, K, tn), lambda j, i: (layer, 0, j)),
            pl.BlockSpec((tm, tn), lambda j, i: (i, j)),
            pl.BlockSpec((1, 1, tn), lambda j, i: (i // tiles_per_group, 0, j)),
        ],
        out_specs=pl.BlockSpec((tm, tn), lambda j, i: (i, j)),
        out_shape=jax.ShapeDtypeStruct((M, D), F32),
        scratch_shapes=[pltpu.VMEM((K, tn), BF16)],
        compiler_params=_cparams("parallel", "arbitrary"),
        name="outproj_residual",
    )(ya, yb, yc, w, x, gate)


def _rope_tables(n_tokens):
    half = HEAD_DIM // 2
    n_freq = half // 2
    t = np.arange(n_tokens)
    pos = np.stack([t // GRID_W, t % GRID_W], axis=1).astype(np.float32)
    d = np.arange(HEAD_DIM)
    which = d // half
    sign = np.where((d % half) < n_freq, -1.0, 1.0).astype(np.float32)
    inv_freq = ROPE_THETA ** (-jnp.arange(n_freq, dtype=F32) / n_freq)
    ang = jnp.asarray(pos)[:, which] * inv_freq[d % n_freq][None, :]
    return jnp.cos(ang), jnp.sin(ang) * sign[None, :]


def _na_geometry(rows):
    n_tiles = rows // NA_TILE_ROWS
    kr = min(NA_ROWS, rows)
    a = np.arange(NA_TILE_ROWS)[:, None]
    r = np.arange(3 * NA_TILE_ROWS)[None, :]
    row_sel = np.zeros((3, NA_TILE_ROWS, 3 * NA_TILE_ROWS, 2 * NA_ROWS - 1), np.float32)
    for v, t in enumerate((0, min(1, n_tiles - 1), n_tiles - 1)):
        qrow = NA_TILE_ROWS * t + a
        krow = NA_TILE_ROWS * (t - 1) + r
        rs = np.clip(qrow - kr // 2, 0, rows - kr)
        ok = (krow >= rs) & (krow < rs + kr) & (krow >= 0) & (krow < rows)
        aa, rr = np.nonzero(ok)
        row_sel[v, aa, rr, (krow - qrow + NA_ROWS - 1)[aa, rr]] = 1.0
    c = np.arange(GRID_W)[:, None]
    d = np.arange(GRID_W)[None, :]
    cs = np.clip(c - NA_COLS // 2, 0, GRID_W - NA_COLS)
    cc, dd = np.nonzero((d >= cs) & (d < cs + NA_COLS))
    col_sel = np.zeros((GRID_W, GRID_W, 2 * NA_COLS - 1), np.float32)
    col_sel[cc, dd, (d - c + NA_COLS - 1)[cc, dd]] = 1.0
    return row_sel, col_sel


def _na_bias(rpb, rows):
    row_sel, col_sel = _na_geometry(rows)
    bias = jnp.einsum('vari,hij,cdj->vhacrd', jnp.asarray(row_sel), rpb, jnp.asarray(col_sel),
                      precision=lax.Precision.HIGHEST)
    valid = ((row_sel.sum(-1) > 0)[:, None, :, None, :, None]
             & (col_sel.sum(-1) > 0)[None, None, None, :, None, :])
    bias = jnp.where(jnp.asarray(valid), bias, NEG_BIG)
    return bias.reshape(3, rpb.shape[0], NA_TQ, 3 * NA_TQ)


def _rope(x, cos, sin, first):
    partner = jnp.where(first, pltpu.roll(x, 3 * HEAD_DIM // 4, 1), pltpu.roll(x, HEAD_DIM // 4, 1))
    return x * cos + partner * sin


def _rope_kv_body(k_ref, v_ref, cos_ref, sin_ref, kr_ref, vb_ref):
    lane = lax.broadcasted_iota(jnp.int32, cos_ref.shape, 1)
    first = (lane % (HEAD_DIM // 2)) < (HEAD_DIM // 4)
    cos, sin = cos_ref[...], sin_ref[...]
    for h in range(k_ref.shape[1] // HEAD_DIM):
        cols = slice(h * HEAD_DIM, (h + 1) * HEAD_DIM)
        kr_ref[:, cols] = _rope(k_ref[:, cols], cos, sin, first).astype(BF16)
    vb_ref[...] = v_ref[...].astype(BF16)


def _head_group(n_heads, *cols):
    g = 4
    while n_heads % g or any(c % g for c in cols):
        g //= 2
    return g


def rope_kv(p, cos_tab, sin_tab, *, batch, n_heads, col_k, tq):
    tiles = p.shape[0] // batch // tq
    g = _head_group(n_heads, col_k, col_k + n_heads)
    blk = (tq, g * HEAD_DIM)
    tab = (tq, HEAD_DIM)
    shape = jax.ShapeDtypeStruct((p.shape[0], n_heads * HEAD_DIM), BF16)
    return pl.pallas_call(
        _rope_kv_body,
        grid=(p.shape[0] // tq, n_heads // g),
        in_specs=[pl.BlockSpec(blk, lambda i, h: (i, col_k // g + h)),
                  pl.BlockSpec(blk, lambda i, h: (i, (col_k + n_heads) // g + h)),
                  pl.BlockSpec(tab, lambda i, h: (i % tiles, 0)),
                  pl.BlockSpec(tab, lambda i, h: (i % tiles, 0))],
        out_specs=[pl.BlockSpec(blk, lambda i, h: (i, h)), pl.BlockSpec(blk, lambda i, h: (i, h))],
        out_shape=[shape, shape],
        compiler_params=_cparams("parallel", "arbitrary"),
        name="rope_kv",
    )(p, p, cos_tab, sin_tab)


def _na_body(q_ref, kp_ref, kc_ref, kn_ref, vp_ref, vc_ref, vn_ref, kx_ref, vx_ref,
             cos_ref, sin_ref, bias_ref, o_ref):
    scale = HEAD_DIM ** -0.5
    lane = lax.broadcasted_iota(jnp.int32, (NA_TQ, HEAD_DIM), 1)
    first = (lane % (HEAD_DIM // 2)) < (HEAD_DIM // 4)
    q = q_ref[...]
    qr = _rope(q, cos_ref[...], sin_ref[...], first).astype(BF16)
    kr = jnp.concatenate([kp_ref[...], kc_ref[...], kn_ref[...]], axis=0)
    s_lat = lax.dot_general(qr, kr, _NT, preferred_element_type=F32) * scale + bias_ref[0, 0]
    s_ctx = lax.dot_general(q.astype(BF16), kx_ref[...].astype(BF16), _NT,
                            preferred_element_type=F32) * scale
    m = jnp.maximum(jnp.max(s_lat, axis=-1, keepdims=True), jnp.max(s_ctx, axis=-1, keepdims=True))
    p_lat = jnp.exp(s_lat - m)
    p_ctx = jnp.exp(s_ctx - m)
    denom = jnp.sum(p_lat, axis=-1, keepdims=True) + jnp.sum(p_ctx, axis=-1, keepdims=True)
    v_all = jnp.concatenate([vp_ref[...], vc_ref[...], vn_ref[...]], axis=0)
    o = (jnp.dot(p_lat.astype(BF16), v_all, preferred_element_type=F32)
         + jnp.dot(p_ctx.astype(BF16), vx_ref[...].astype(BF16), preferred_element_type=F32))
    o_ref[...] = (o / denom).astype(o_ref.dtype)


def neighbourhood_attention(p, pc, kr, vb, bias, cos_tab, sin_tab, *, batch, n_heads, col_q, ctx_len):
    n_tok = p.shape[0] // batch
    n_tiles = n_tok // NA_TQ
    col_k, col_v = col_q + n_heads, col_q + 2 * n_heads
    last = n_tiles - 1

    def tile(h, b, t):
        return b * n_tiles + t

    def prev(h, b, t):
        return b * n_tiles + jnp.maximum(t - 1, 0)

    def nxt(h, b, t):
        return b * n_tiles + jnp.minimum(t + 1, last)

    def variant(h, b, t):
        return jnp.where(t == 0, 0, jnp.where(t == last, 2, 1))

    blk = (NA_TQ, HEAD_DIM)
    ctx_blk = (ctx_len, HEAD_DIM)
    in_specs = [
        pl.BlockSpec(blk, lambda h, b, t: (tile(h, b, t), col_q + h)),
        pl.BlockSpec(blk, lambda h, b, t: (prev(h, b, t), h)),
        pl.BlockSpec(blk, lambda h, b, t: (tile(h, b, t), h)),
        pl.BlockSpec(blk, lambda h, b, t: (nxt(h, b, t), h)),
        pl.BlockSpec(blk, lambda h, b, t: (prev(h, b, t), h)),
        pl.BlockSpec(blk, lambda h, b, t: (tile(h, b, t), h)),
        pl.BlockSpec(blk, lambda h, b, t: (nxt(h, b, t), h)),
        pl.BlockSpec(ctx_blk, lambda h, b, t: (b, col_k + h)),
        pl.BlockSpec(ctx_blk, lambda h, b, t: (b, col_v + h)),
        pl.BlockSpec(blk, lambda h, b, t: (t, 0)),
        pl.BlockSpec(blk, lambda h, b, t: (t, 0)),
        pl.BlockSpec((1, 1, NA_TQ, 3 * NA_TQ), lambda h, b, t: (variant(h, b, t), h, 0, 0)),
    ]
    return pl.pallas_call(
        _na_body,
        grid=(n_heads, batch, n_tiles),
        in_specs=in_specs,
        out_specs=pl.BlockSpec(blk, lambda h, b, t: (tile(h, b, t), h)),
        out_shape=jax.ShapeDtypeStruct((p.shape[0], n_heads * HEAD_DIM), BF16),
        compiler_params=_cparams("parallel", "parallel", "arbitrary"),
        name="neighbourhood_attention",
    )(p, kr, kr, kr, vb, vb, vb, pc, pc, cos_tab, sin_tab, bias)


def _ctx_attn_body(q_ref, k_ref, v_ref, o_ref):
    nt = (((1,), (1,)), ((), ()))
    s = lax.dot_general(q_ref[...].astype(BF16), k_ref[...].astype(BF16), nt,
                        preferred_element_type=F32) * HEAD_DIM ** -0.5
    p = jnp.exp(s - jnp.max(s, axis=-1, keepdims=True))
    denom = jnp.sum(p, axis=-1, keepdims=True)
    o = jnp.dot(p.astype(BF16), v_ref[...].astype(BF16), preferred_element_type=F32)
    o_ref[...] = (o / denom).astype(o_ref.dtype)


def context_attention(pc, *, batch, n_heads, col_q, ctx_len):
    col_k, col_v = col_q + n_heads, col_q + 2 * n_heads
    blk = (ctx_len, HEAD_DIM)
    return pl.pallas_call(
        _ctx_attn_body,
        grid=(batch, n_heads),
        in_specs=[pl.BlockSpec(blk, lambda b, h: (b, col_q + h)),
                  pl.BlockSpec(blk, lambda b, h: (b, col_k + h)),
                  pl.BlockSpec(blk, lambda b, h: (b, col_v + h))],
        out_specs=pl.BlockSpec(blk, lambda b, h: (b, h)),
        out_shape=jax.ShapeDtypeStruct((pc.shape[0], n_heads * HEAD_DIM), BF16),
        compiler_params=_cparams("parallel", "parallel"),
        name="context_attention",
    )(pc, pc, pc)


def _conv_body(val_ref, gate_ref, pval_ref, pgate_ref, nval_ref, ngate_ref, w_ref, b_ref, lg_ref, lb_ref,
               o_ref, u_scr, *, tt):
    t = pl.program_id(1)

    def glu(v_ref, g_ref):
        return v_ref[...] * jax.nn.sigmoid(g_ref[...])

    span = tt + 2 * CONV_HALO
    u_scr[0, 0:CONV_HALO, :] = jnp.where(t > 0, glu(pval_ref, pgate_ref), 0.0)
    u_scr[0, CONV_HALO:CONV_HALO + tt, :] = glu(val_ref, gate_ref)
    u_scr[0, CONV_HALO + tt:span, :] = jnp.where(t < pl.num_programs(1) - 1, glu(nval_ref, ngate_ref), 0.0)
    u_scr[0, span:, :] = jnp.zeros((SUBLANES, u_scr.shape[2]), F32)
    for a in range(1, SUBLANES):
        u_scr[a, 0:span, :] = u_scr[0, a:a + span, :]
    off = CONV_HALO - CONV_K // 2

    def tap(r, k):
        start = r + off + k
        a = start % SUBLANES
        return w_ref[k:k + 1, :] * u_scr[a, start - a:start - a + CONV_ROWS, :]

    for r in range(0, tt, CONV_ROWS):
        acc = tap(r, 0)
        for k in range(1, CONV_K):
            acc = acc + tap(r, k)
        acc = acc + b_ref[...]
        mu = jnp.mean(acc, axis=-1, keepdims=True)
        cen = acc - mu
        var = jnp.mean(cen * cen, axis=-1, keepdims=True)
        un = cen * lax.rsqrt(var + NORM_EPS) * lg_ref[...] + lb_ref[...]
        o_ref[r:r + CONV_ROWS, :] = (un * jax.nn.sigmoid(un)).astype(o_ref.dtype)


def conformer_conv(p, w_dw, b_dw, ln_g, ln_b, *, batch, tt):
    C = w_dw.shape[1]
    T = p.shape[0] // batch
    nt = T // tt
    hb = tt // CONV_HALO
    seq_hb = T // CONV_HALO

    def prev_blk(b, t):
        return jnp.maximum(b * seq_hb + t * hb - 1, 0)

    def next_blk(b, t):
        return jnp.minimum(b * seq_hb + (t + 1) * hb, batch * seq_hb - 1)

    main = (tt, C)
    halo = (CONV_HALO, C)
    vec = pl.BlockSpec((1, C), lambda b, t: (0, 0))
    return pl.pallas_call(
        functools.partial(_conv_body, tt=tt),
        grid=(batch, nt),
        in_specs=[pl.BlockSpec(main, lambda b, t: (b * nt + t, 0)),
                  pl.BlockSpec(main, lambda b, t: (b * nt + t, 1)),
                  pl.BlockSpec(halo, lambda b, t: (prev_blk(b, t), 0)),
                  pl.BlockSpec(halo, lambda b, t: (prev_blk(b, t), 1)),
                  pl.BlockSpec(halo, lambda b, t: (next_blk(b, t), 0)),
                  pl.BlockSpec(halo, lambda b, t: (next_blk(b, t), 1)),
                  pl.BlockSpec((CONV_K, C), lambda b, t: (0, 0)), vec, vec, vec],
        out_specs=pl.BlockSpec(main, lambda b, t: (b * nt + t, 0)),
        out_shape=jax.ShapeDtypeStruct((p.shape[0], C), BF16),
        scratch_shapes=[pltpu.VMEM((SUBLANES, tt + 2 * CONV_HALO + SUBLANES, C), F32)],
        compiler_params=_cparams("parallel", "parallel"),
        name="conformer_conv",
    )(p, p, p, p, p, p, w_dw, b_dw, ln_g, ln_b)


_NT = (((1,), (1,)), ((), ()))
_TN = (((0,), (0,)), ((), ()))


def _hg_gates(zq, zf, lb):
    q = zq * jax.nn.sigmoid(zq) * HEAD_DIM ** -0.5
    f = lb + (1.0 - lb) * jax.nn.sigmoid(zf)
    log_f = jnp.log(jnp.maximum(f, F_MIN))
    k = (1.0 - lb) * jax.nn.sigmoid(-zf)
    return q, k, log_f


def _block_cumsum(tri, g):
    out = []
    for r in range(g.shape[0] // tri.shape[0]):
        x = g[r * tri.shape[0]:(r + 1) * tri.shape[0], :]
        hi = x.astype(BF16)
        rest = x - hi.astype(F32)
        mid = rest.astype(BF16)
        lo = (rest - mid.astype(F32)).astype(BF16)
        out.append(jnp.dot(tri, hi, preferred_element_type=F32)
                   + jnp.dot(tri, mid, preferred_element_type=F32)
                   + jnp.dot(tri, lo, preferred_element_type=F32))
    return jnp.concatenate(out, axis=0)


def _hg_factors(d, r, L, q_s, k_s, b_s, fac_s):
    rows = slice(r, r + L)
    mid = r + L // 2 - 1
    end = r if d == 1 else r + L - 1
    b, q, k = b_s[d, rows, :], q_s[d, rows, :], k_s[d, rows, :]
    bm = b - b_s[d, mid:mid + 1, :]
    fac_s[0, d, rows, :] = (q * jnp.exp(bm)).astype(BF16)
    fac_s[1, d, rows, :] = (k * jnp.exp(-bm)).astype(BF16)
    fac_s[2, d, rows, :] = (q * jnp.exp(b)).astype(BF16)
    fac_s[3, d, rows, :] = (k * jnp.exp(b_s[d, end:end + 1, :] - b)).astype(BF16)


def _hg_chunk(r, d, st, fac_s, b_s, v_ref, o_ref, reverse, L):
    rows = slice(r, r + L)
    qa, ka, qi, kd = (fac_s[i, d, rows, :] for i in range(4))
    vI = v_ref[rows, :].astype(BF16)
    att = lax.dot_general(qa, ka, _NT, preferred_element_type=F32)
    ti = lax.broadcasted_iota(jnp.int32, (L, L), 0)
    si = lax.broadcasted_iota(jnp.int32, (L, L), 1)
    att = jnp.where((si >= ti) if reverse else (si <= ti), att, 0.0)
    o_ref[rows, :] = (jnp.dot(att.astype(BF16), vI, preferred_element_type=F32)
                      + lax.dot_general(qi, st.astype(BF16), _NT, preferred_element_type=F32))
    end = r if reverse else r + L - 1
    upd = lax.dot_general(vI, kd, _TN, preferred_element_type=F32)
    return st * jnp.exp(b_s[d, end:end + 1, :]) + upd


def _hg_sub_block(r, d, q_s, k_s, b_s, v_ref, o_ref, st_s, reverse):
    nt, tn = _NT, _TN
    bI = b_s[d, pl.ds(r, HG_SUB), :]
    qI = q_s[d, pl.ds(r, HG_SUB), :]
    kI = k_s[d, pl.ds(r, HG_SUB), :]
    vI = v_ref[pl.ds(r, HG_SUB), :]
    st = st_s[d]
    o_inter = lax.dot_general((qI * jnp.exp(bI)).astype(BF16), st.astype(BF16), nt,
                              preferred_element_type=F32)
    sub = lax.broadcasted_iota(jnp.int32, (HG_SUB, HEAD_DIM), 0)
    slabs = []
    for t in range(HG_SUB):
        keep = (sub >= t) if reverse else (sub <= t)
        decay = jnp.where(keep, jnp.exp(bI[t:t + 1, :] - bI), 0.0)
        slabs.append(decay * (qI[t:t + 1, :] * kI))
    x3 = jnp.concatenate(slabs, axis=0).astype(BF16)
    att = jnp.dot(x3, jnp.ones((HEAD_DIM, HEAD_DIM), BF16), preferred_element_type=F32)
    z = att * jnp.concatenate([vI] * HG_SUB, axis=0)
    o_intra = jnp.sum(z.reshape(HG_SUB, HG_SUB, HEAD_DIM), axis=1)
    o_ref[pl.ds(r, HG_SUB), :] = o_inter + o_intra
    b_end = bI[0:1, :] if reverse else bI[HG_SUB - 1:HG_SUB, :]
    k_dec = (kI * jnp.exp(b_end - bI)).astype(BF16)
    upd = lax.dot_general(vI.astype(BF16), k_dec, tn, preferred_element_type=F32)
    st_s[d] = st * jnp.exp(b_end) + upd


def _hg_body(zqf_ref, zvf_ref, zff_ref, zqb_ref, zvb_ref, zfb_ref, lbf_ref, lbb_ref,
             tri_ref, s0_ref, of_ref, ob_ref, sout_ref,
             q_s, k_s, g_s, b_s, fac_s, st_s, *, tb):
    c = pl.program_id(1)

    @pl.when(c == 0)
    def _():
        st_s[...] = s0_ref[...]

    for d, (zq_ref, zf_ref, lb_ref) in enumerate(((zqf_ref, zff_ref, lbf_ref), (zqb_ref, zfb_ref, lbb_ref))):
        q_s[d], k_s[d], g_s[d] = _hg_gates(zq_ref[...], zf_ref[...], lb_ref[...])

    def cumulate(level):
        for d in range(2):
            b_s[d] = _block_cumsum(tri_ref[2 * level + d], g_s[d])

    def spread(L):
        dev = None
        for d in range(2):
            for r in range(0, tb, L):
                mid = r + L // 2 - 1
                dev_r = jnp.abs(b_s[d, r:r + L, :] - b_s[d, mid:mid + 1, :])
                dev = dev_r if dev is None else jnp.maximum(dev, dev_r)
        return jnp.max(dev)

    def factored(L):
        for d in range(2):
            for r in range(0, tb, L):
                _hg_factors(d, r, L, q_s, k_s, b_s, fac_s)
        n = tb // L
        st_f, st_b = st_s[0], st_s[1]
        for i in range(n):
            st_f = _hg_chunk(i * L, 0, st_f, fac_s, b_s, zvf_ref, of_ref, False, L)
            st_b = _hg_chunk((n - 1 - i) * L, 1, st_b, fac_s, b_s, zvb_ref, ob_ref, True, L)
        st_s[0] = st_f
        st_s[1] = st_b

    def dispatch(level):
        cumulate(level)
        if level == len(HG_CHUNKS):
            exact()
            return
        safe = spread(HG_CHUNKS[level]) <= HG_SAFE_DECAY
        pl.when(safe)(lambda: factored(HG_CHUNKS[level]))
        pl.when(jnp.logical_not(safe))(lambda: dispatch(level + 1))

    def exact():
        n = tb // HG_SUB

        def step(i, carry):
            rf = pl.multiple_of(i * HG_SUB, HG_SUB)
            _hg_sub_block(rf, 0, q_s, k_s, b_s, zvf_ref, of_ref, st_s, False)
            rb = pl.multiple_of((n - 1 - i) * HG_SUB, HG_SUB)
            _hg_sub_block(rb, 1, q_s, k_s, b_s, zvb_ref, ob_ref, st_s, True)
            return carry

        lax.fori_loop(0, n, step, 0)

    dispatch(0)

    @pl.when(c == pl.num_programs(1) - 1)
    def _():
        sout_ref[...] = st_s[...]


def hgrn2_bidir(p, lb_f, lb_b, s0, *, batch, n_heads, col0, tb):
    n_tok = p.shape[0] // batch
    nb = n_tok // tb
    H = n_heads
    tri = []
    for size in HG_CHUNKS + (HG_SUB,):
        eye = np.kron(np.eye(HG_TRI // size), np.ones((size, size)))
        tri += [np.tril(eye), np.triu(eye)]
    tri = jnp.asarray(np.stack(tri), BF16)

    def fwd(col):
        return pl.BlockSpec((tb, HEAD_DIM), lambda bh, c: ((bh // H) * nb + c, col + bh % H))

    def bwd(col):
        return pl.BlockSpec((tb, HEAD_DIM), lambda bh, c: ((bh // H) * nb + nb - 1 - c, col + bh % H))

    def out(rev):
        if rev:
            return pl.BlockSpec((tb, HEAD_DIM), lambda bh, c: ((bh // H) * nb + nb - 1 - c, bh % H))
        return pl.BlockSpec((tb, HEAD_DIM), lambda bh, c: ((bh // H) * nb + c, bh % H))

    lb_spec = pl.BlockSpec((None, 1, HEAD_DIM), lambda bh, c: (bh % H, 0, 0))
    tri_spec = pl.BlockSpec(tri.shape, lambda bh, c: (0, 0, 0))
    st_spec = pl.BlockSpec((None, 2, HEAD_DIM, HEAD_DIM), lambda bh, c: (bh, 0, 0, 0))
    o_shape = jax.ShapeDtypeStruct((p.shape[0], H * HEAD_DIM), F32)
    dir_buf = pltpu.VMEM((2, tb, HEAD_DIM), F32)
    return pl.pallas_call(
        functools.partial(_hg_body, tb=tb),
        grid=(batch * H, nb),
        in_specs=[fwd(col0), fwd(col0 + H), fwd(col0 + 2 * H),
                  bwd(col0), bwd(col0 + H), bwd(col0 + 3 * H),
                  lb_spec, lb_spec, tri_spec, st_spec],
        out_specs=[out(False), out(True), st_spec],
        out_shape=[o_shape, o_shape, jax.ShapeDtypeStruct(s0.shape, F32)],
        scratch_shapes=[dir_buf, dir_buf, dir_buf, dir_buf, pltpu.VMEM((4, 2, tb, HEAD_DIM), BF16),
                        pltpu.VMEM((2, HEAD_DIM, HEAD_DIM), F32)],
        compiler_params=_cparams("parallel", "arbitrary"),
        name="hgrn2_bidir",
    )(p, p, p, p, p, p, lb_f, lb_b, tri, s0)


def _hg_out_body(of_ref, ob_ref, g_ref, gain_ref, y_ref):
    for h in range(of_ref.shape[1] // HEAD_DIM):
        cols = slice(h * HEAD_DIM, (h + 1) * HEAD_DIM)
        o = of_ref[:, cols] + ob_ref[:, cols]
        on = o * lax.rsqrt(jnp.mean(o * o, axis=-1, keepdims=True) + NORM_EPS) * gain_ref[:, cols]
        g = g_ref[:, cols]
        y_ref[:, cols] = (on * (g * jax.nn.sigmoid(g))).astype(y_ref.dtype)


def hgrn2_output(o_f, o_b, p, gain, *, n_heads, col_g, tm):
    M = o_f.shape[0]
    grp = _head_group(n_heads, col_g)
    blk = (tm, grp * HEAD_DIM)
    return pl.pallas_call(
        _hg_out_body,
        grid=(M // tm, n_heads // grp),
        in_specs=[pl.BlockSpec(blk, lambda i, h: (i, h)),
                  pl.BlockSpec(blk, lambda i, h: (i, h)),
                  pl.BlockSpec(blk, lambda i, h: (i, col_g // grp + h)),
                  pl.BlockSpec((1, grp * HEAD_DIM), lambda i, h: (0, h))],
        out_specs=pl.BlockSpec(blk, lambda i, h: (i, h)),
        out_shape=jax.ShapeDtypeStruct(o_f.shape, BF16),
        compiler_params=_cparams("parallel", "parallel"),
        name="hgrn2_output",
    )(o_f, o_b, p, gain)


def _ffn_body(x_ref, wg_ref, wu_ref, wd_ref, gate_ref, res_gate_ref, o_ref):
    x = x_ref[...]
    u = jnp.dot(x, wg_ref[...], preferred_element_type=F32)
    w = jnp.dot(x, wu_ref[...], preferred_element_type=F32)
    a = (u * jax.nn.sigmoid(u) * w).astype(BF16)
    o_ref[...] = jnp.dot(a, wd_ref[...], preferred_element_type=F32) * gate_ref[...] * res_gate_ref[0]


def expert_ffn(xe, w_gate, w_up, w_down, layer, gate, res_gate, *, tm):
    B, E, cap, D = xe.shape
    F = w_gate.shape[-1]
    per_sample = res_gate.shape[0] > 1
    return pl.pallas_call(
        _ffn_body,
        grid=(E, B, cap // tm),
        in_specs=[pl.BlockSpec((None, None, tm, D), lambda e, b, r: (b, e, r, 0)),
                  pl.BlockSpec((None, None, D, F), lambda e, b, r: (layer, e, 0, 0)),
                  pl.BlockSpec((None, None, D, F), lambda e, b, r: (layer, e, 0, 0)),
                  pl.BlockSpec((None, None, F, D), lambda e, b, r: (layer, e, 0, 0)),
                  pl.BlockSpec((None, None, tm, 1), lambda e, b, r: (b, e, r, 0)),
                  pl.BlockSpec((1, 1, D), lambda e, b, r: (b if per_sample else 0, 0, 0))],
        out_specs=pl.BlockSpec((None, None, tm, D), lambda e, b, r: (b, e, r, 0)),
        out_shape=jax.ShapeDtypeStruct((B, E, cap, D), F32),
        compiler_params=_cparams("parallel", "parallel", "arbitrary"),
        name="expert_ffn",
    )(xe, w_gate, w_up, w_down, gate, res_gate)


def _expert_choice_ffn(x, h, logits, res_gate, w_gate, w_up, w_down, layer):
    B, n, D = x.shape
    cap = EC_CAPACITY * n // N_EXPERTS
    aff = jax.nn.softmax(logits, axis=-1)
    gate, idx = lax.top_k(aff.transpose(0, 2, 1), cap)
    bi = jnp.arange(B)[:, None, None]
    xe = h[bi, idx]
    y = expert_ffn(xe, w_gate, w_up, w_down, layer, gate[..., None], res_gate, tm=min(cap, 512))
    return x.at[bi, idx].add(y)


def kernel(x, c, ctx, c_ctx, mod_down, mod_up, mod_bias, mix_norm, ffn_norm, w_in, w_out,
           conv_w, conv_b, conv_ln_g, conv_ln_b, na_rpb, hg_lb_logits, hg_norm, router,
           w_gate, w_up, w_down, final_norm):
    B, N, D = x.shape
    M = ctx.shape[1]
    depth = w_in.shape[0]
    rows = N // GRID_W
    w_conv = conv_w.shape[-1]
    w_hg = hg_norm.shape[-1]
    n_hg = w_hg // HEAD_DIM
    n_na = na_rpb.shape[1]
    w_na = n_na * HEAD_DIM
    in_conv = 2 * w_conv
    col_na = in_conv // HEAD_DIM
    col_hg = (in_conv + 3 * w_na) // HEAD_DIM
    n_exp = router.shape[-1]

    lb_p = jax.nn.softmax(hg_lb_logits.astype(F32), axis=1)
    lb_all = jnp.cumsum(lb_p, axis=1) - lb_p[:, :1]
    a_lat = jax.nn.silu(c)
    a_ctx = jax.nn.silu(c_ctx)
    cos_tab, sin_tab = _rope_tables(N)
    s_zero = jnp.zeros((B * n_hg, 2, HEAD_DIM, HEAD_DIM), F32)
    experts = (w_gate.astype(BF16), w_up.astype(BF16), w_down.astype(BF16))

    xl = x.reshape(B * N, D)
    xc = ctx.reshape(B * M, D)
    for l in range(depth):
        last = l == depth - 1
        hi = lax.Precision.HIGHEST
        mod = jnp.dot(jnp.dot(a_lat, mod_down[l], precision=hi), mod_up[l], precision=hi) + mod_bias[l]
        mod_c = jnp.dot(jnp.dot(a_ctx, mod_down[l], precision=hi), mod_up[l], precision=hi) + mod_bias[l]
        sh1, sc1, g1, sh2, sc2, g2 = [m[:, None, :] for m in jnp.split(mod, N_MOD, axis=-1)]
        csh1, csc1, cg1, csh2, csc2, cg2 = [m[None, None, :] for m in jnp.split(mod_c, N_MOD, axis=-1)]

        g_mix = mix_norm[l][None, :]
        g_ffn = ffn_norm[l][None, :]
        router_l = jnp.pad(router[l], ((0, 0), (0, ROUTER_PAD - n_exp)))
        conv_args = (conv_w[l], conv_b[l][None, :], conv_ln_g[l][None, :], conv_ln_b[l][None, :])

        tn_in = _tile(w_in.shape[2], 512)
        tn_out = _tile(D, 512)
        tm_lat = _tile(N, 1024)
        h1 = norm_mod(xl, g_mix, sc1, sh1, tm=512)
        hc1 = norm_mod(xc, g_mix, csc1, csh1, tm=B * M)
        p = matmul_wcast(h1, w_in, l, tm=tm_lat, tn=tn_in)
        pc = matmul_wcast(hc1, w_in, l, tm=B * M, tn=tn_in)

        ya = conformer_conv(p, *conv_args, batch=B, tt=_tile(N, 512))
        bias = _na_bias(na_rpb[l], rows)
        kr, vb = rope_kv(p, cos_tab, sin_tab, batch=B, n_heads=n_na, col_k=col_na + n_na, tq=_tile(N, 512))
        yb = neighbourhood_attention(p, pc, kr, vb, bias, cos_tab, sin_tab, batch=B, n_heads=n_na,
                                     col_q=col_na, ctx_len=M)
        lb_f = lb_all[0, l].reshape(n_hg, 1, HEAD_DIM)
        lb_b = lb_all[1, l].reshape(n_hg, 1, HEAD_DIM)
        oc_f, oc_b, s_ctx = hgrn2_bidir(pc, lb_f, lb_b, s_zero, batch=B, n_heads=n_hg, col0=col_hg, tb=M)
        o_f, o_b, _ = hgrn2_bidir(p, lb_f, lb_b, s_ctx, batch=B, n_heads=n_hg, col0=col_hg, tb=HG_TB)
        gain = hg_norm[l][None, :]
        yc = hgrn2_output(o_f, o_b, p, gain, n_heads=n_hg, col_g=col_hg + 4 * n_hg, tm=512)

        xl = outproj_residual(ya, yb, yc, w_out, l, xl, g1, tm=tm_lat, tn=tn_out)
        h2, logits = norm_mod(xl, g_ffn, sc2, sh2, tm=512, router=router_l)
        xl = _expert_choice_ffn(xl.reshape(B, N, D), h2.reshape(B, N, D),
                                logits[:, :n_exp].reshape(B, N, n_exp), g2, *experts, l).reshape(B * N, D)

        if not last:
            yac = conformer_conv(pc, *conv_args, batch=B, tt=M)
            ybc = context_attention(pc, batch=B, n_heads=n_na, col_q=col_na, ctx_len=M)
            ycc = hgrn2_output(oc_f, oc_b, pc, gain, n_heads=n_hg, col_g=col_hg + 4 * n_hg, tm=B * M)
            xc = outproj_residual(yac, ybc, ycc, w_out, l, xc, cg1, tm=B * M, tn=tn_out)
            hc2, logits_c = norm_mod(xc, g_ffn, csc2, csh2, tm=B * M, router=router_l)
            xc = _expert_choice_ffn(xc.reshape(B, M, D), hc2.reshape(B, M, D),
                                    logits_c[:, :n_exp].reshape(B, M, n_exp), cg2, *experts, l).reshape(B * M, D)

    zero = jnp.zeros((1, 1, D), F32)
    out = norm_mod(xl, final_norm[None, :], zero, zero, tm=512, out_dtype=F32)
    return out.reshape(B, N, D)
```

```python
import functools

import numpy as np
import jax
import jax.numpy as jnp
from jax import lax
from jax.experimental import pallas as pl
from jax.experimental.pallas import tpu as pltpu

GRID_W = 64
HEAD_DIM = 128
CONV_K = 31
NA_ROWS = 8
NA_COLS = 16
ROPE_THETA = 10000.0
F_MIN = 1e-6
N_EXPERTS = 16
EC_CAPACITY = 2
N_MOD = 6
NORM_EPS = 1e-6

BF16 = jnp.bfloat16
F32 = jnp.float32

VMEM_LIMIT_BYTES = 56 * 1024 * 1024
NEG_BIG = -1e30
NA_TILE_ROWS = 4
NA_TQ = NA_TILE_ROWS * GRID_W
SUBLANES = 8
CONV_HALO = 16
CONV_ROWS = 32
ROUTER_PAD = 128
HG_SUB = 16
HG_CHUNKS = (128, 64)
HG_SAFE_DECAY = 84.0
HG_TRI = 128
HG_TB = 512


def _cparams(*sem):
    return pltpu.CompilerParams(dimension_semantics=sem, vmem_limit_bytes=VMEM_LIMIT_BYTES)


def _tile(n, preferred):
    t = preferred
    while n % t:
        t //= 2
    return t


def _norm_mod(x, g, sc, sh):
    r = lax.rsqrt(jnp.mean(x * x, axis=-1, keepdims=True) + NORM_EPS)
    return (x * r * g) * (1.0 + sc) + sh


def _nm_body(x_ref, g_ref, sc_ref, sh_ref, *rest):
    h = _norm_mod(x_ref[...], g_ref[...], sc_ref[0], sh_ref[0])
    if len(rest) == 3:
        r_ref, o_ref, logit_ref = rest
        h_hi = h.astype(BF16)
        h_lo = (h - h_hi.astype(F32)).astype(BF16)
        logit_ref[...] = (jnp.dot(h_hi, r_ref[0], preferred_element_type=F32)
                          + jnp.dot(h_hi, r_ref[1], preferred_element_type=F32)
                          + jnp.dot(h_lo, r_ref[0], preferred_element_type=F32))
    else:
        o_ref, = rest
    o_ref[...] = h.astype(o_ref.dtype)


def norm_mod(x, g, sc, sh, *, tm, router=None, out_dtype=BF16):
    M, D = x.shape
    G = sc.shape[0]
    tiles_per_group = M // G // tm
    in_specs = [
        pl.BlockSpec((tm, D), lambda i: (i, 0)),
        pl.BlockSpec((1, D), lambda i: (0, 0)),
        pl.BlockSpec((1, 1, D), lambda i: (i // tiles_per_group, 0, 0)),
        pl.BlockSpec((1, 1, D), lambda i: (i // tiles_per_group, 0, 0)),
    ]
    out_specs = pl.BlockSpec((tm, D), lambda i: (i, 0))
    out_shape = jax.ShapeDtypeStruct((M, D), out_dtype)
    args = (x, g, sc, sh)
    if router is not None:
        E = router.shape[2]
        in_specs.append(pl.BlockSpec((2, D, E), lambda i: (0, 0, 0)))
        out_specs = [out_specs, pl.BlockSpec((tm, E), lambda i: (i, 0))]
        out_shape = [out_shape, jax.ShapeDtypeStruct((M, E), F32)]
        args += (router,)
    return pl.pallas_call(
        _nm_body,
        grid=(M // tm,),
        in_specs=in_specs,
        out_specs=out_specs,
        out_shape=out_shape,
        compiler_params=_cparams("parallel"),
        name="norm_mod",
    )(*args)


def _mm_body(h_ref, w_ref, o_ref, wb_scr):
    @pl.when(pl.program_id(1) == 0)
    def _():
        wb_scr[...] = w_ref[...].astype(BF16)

    o_ref[...] = jnp.dot(h_ref[...], wb_scr[...], preferred_element_type=F32)


def matmul_wcast(h, w, layer, *, tm, tn):
    M, K = h.shape
    N = w.shape[2]
    return pl.pallas_call(
        _mm_body,
        grid=(N // tn, M // tm),
        in_specs=[pl.BlockSpec((tm, K), lambda j, i: (i, 0)),
                  pl.BlockSpec((None, K, tn), lambda j, i: (layer, 0, j))],
        out_specs=pl.BlockSpec((tm, tn), lambda j, i: (i, j)),
        out_shape=jax.ShapeDtypeStruct((M, N), F32),
        scratch_shapes=[pltpu.VMEM((K, tn), BF16)],
        compiler_params=_cparams("parallel", "arbitrary"),
        name="matmul_wcast",
    )(h, w)


def _outproj_body(ya_ref, yb_ref, yc_ref, w_ref, x_ref, g_ref, o_ref, wb_scr):
    @pl.when(pl.program_id(1) == 0)
    def _():
        wb_scr[...] = w_ref[...].astype(BF16)

    ka, kb = ya_ref.shape[1], yb_ref.shape[1]
    acc = jnp.dot(ya_ref[...], wb_scr[0:ka, :], preferred_element_type=F32)
    acc += jnp.dot(yb_ref[...], wb_scr[ka:ka + kb, :], preferred_element_type=F32)
    acc += jnp.dot(yc_ref[...], wb_scr[ka + kb:, :], preferred_element_type=F32)
    o_ref[...] = x_ref[...] + g_ref[0] * acc


def outproj_residual(ya, yb, yc, w, layer, x, gate, *, tm, tn):
    M, D = x.shape
    G = gate.shape[0]
    tiles_per_group = M // G // tm
    ka, kb, kc = ya.shape[1], yb.shape[1], yc.shape[1]
    K = ka + kb + kc
    return pl.pallas_call(
        _outproj_body,
        grid=(D // tn, M // tm),
        in_specs=[
            pl.BlockSpec((tm, ka), lambda j, i: (i, 0)),
            pl.BlockSpec((tm, kb), lambda j, i: (i, 0)),
            pl.BlockSpec((tm, kc), lambda j, i: (i, 0)),
            pl.BlockSpec((None, K, tn), lambda j, i: (layer, 0, j)),
            pl.BlockSpec((tm, tn), lambda j, i: (i, j)),
            pl.BlockSpec((1, 1, tn), lambda j, i: (i // tiles_per_group, 0, j)),
        ],
        out_specs=pl.BlockSpec((tm, tn), lambda j, i: (i, j)),
        out_shape=jax.ShapeDtypeStruct((M, D), F32),
        scratch_shapes=[pltpu.VMEM((K, tn), BF16)],
        compiler_params=_cparams("parallel", "arbitrary"),
        name="outproj_residual",
    )(ya, yb, yc, w, x, gate)


def _rope_tables(n_tokens):
    half = HEAD_DIM // 2
    n_freq = half // 2
    t = np.arange(n_tokens)
    pos = np.stack([t // GRID_W, t % GRID_W], axis=1).astype(np.float32)
    d = np.arange(HEAD_DIM)
    which = d // half
    sign = np.where((d % half) < n_freq, -1.0, 1.0).astype(np.float32)
    inv_freq = ROPE_THETA ** (-jnp.arange(n_freq, dtype=F32) / n_freq)
    ang = jnp.asarray(pos)[:, which] * inv_freq[d % n_freq][None, :]
    return jnp.cos(ang), jnp.sin(ang) * sign[None, :]


def _na_geometry(rows):
    n_tiles = rows // NA_TILE_ROWS
    kr = min(NA_ROWS, rows)
    a = np.arange(NA_TILE_ROWS)[:, None]
    r = np.arange(3 * NA_TILE_ROWS)[None, :]
    row_sel = np.zeros((3, NA_TILE_ROWS, 3 * NA_TILE_ROWS, 2 * NA_ROWS - 1), np.float32)
    for v, t in enumerate((0, min(1, n_tiles - 1), n_tiles - 1)):
        qrow = NA_TILE_ROWS * t + a
        krow = NA_TILE_ROWS * (t - 1) + r
        rs = np.clip(qrow - kr // 2, 0, rows - kr)
        ok = (krow >= rs) & (krow < rs + kr) & (krow >= 0) & (krow < rows)
        aa, rr = np.nonzero(ok)
        row_sel[v, aa, rr, (krow - qrow + NA_ROWS - 1)[aa, rr]] = 1.0
    c = np.arange(GRID_W)[:, None]
    d = np.arange(GRID_W)[None, :]
    cs = np.clip(c - NA_COLS // 2, 0, GRID_W - NA_COLS)
    cc, dd = np.nonzero((d >= cs) & (d < cs + NA_COLS))
    col_sel = np.zeros((GRID_W, GRID_W, 2 * NA_COLS - 1), np.float32)
    col_sel[cc, dd, (d - c + NA_COLS - 1)[cc, dd]] = 1.0
    return row_sel, col_sel


def _na_bias(rpb, rows):
    row_sel, col_sel = _na_geometry(rows)
    bias = jnp.einsum('vari,hij,cdj->vhacrd', jnp.asarray(row_sel), rpb, jnp.asarray(col_sel),
                      precision=lax.Precision.HIGHEST)
    valid = ((row_sel.sum(-1) > 0)[:, None, :, None, :, None]
             & (col_sel.sum(-1) > 0)[None, None, None, :, None, :])
    bias = jnp.where(jnp.asarray(valid), bias, NEG_BIG)
    return bias.reshape(3, rpb.shape[0], NA_TQ, 3 * NA_TQ)


def _rope(x, cos, sin, first):
    partner = jnp.where(first, pltpu.roll(x, 3 * HEAD_DIM // 4, 1), pltpu.roll(x, HEAD_DIM // 4, 1))
    return x * cos + partner * sin


def _rope_kv_body(k_ref, v_ref, cos_ref, sin_ref, kr_ref, vb_ref):
    lane = lax.broadcasted_iota(jnp.int32, cos_ref.shape, 1)
    first = (lane % (HEAD_DIM // 2)) < (HEAD_DIM // 4)
    cos, sin = cos_ref[...], sin_ref[...]
    for h in range(k_ref.shape[1] // HEAD_DIM):
        cols = slice(h * HEAD_DIM, (h + 1) * HEAD_DIM)
        kr_ref[:, cols] = _rope(k_ref[:, cols], cos, sin, first).astype(BF16)
    vb_ref[...] = v_ref[...].astype(BF16)


def _head_group(n_heads, *cols):
    g = 4
    while n_heads % g or any(c % g for c in cols):
        g //= 2
    return g


def rope_kv(p, cos_tab, sin_tab, *, batch, n_heads, col_k, tq):
    tiles = p.shape[0] // batch // tq
    g = _head_group(n_heads, col_k, col_k + n_heads)
    blk = (tq, g * HEAD_DIM)
    tab = (tq, HEAD_DIM)
    shape = jax.ShapeDtypeStruct((p.shape[0], n_heads * HEAD_DIM), BF16)
    return pl.pallas_call(
        _rope_kv_body,
        grid=(p.shape[0] // tq, n_heads // g),
        in_specs=[pl.BlockSpec(blk, lambda i, h: (i, col_k // g + h)),
                  pl.BlockSpec(blk, lambda i, h: (i, (col_k + n_heads) // g + h)),
                  pl.BlockSpec(tab, lambda i, h: (i % tiles, 0)),
                  pl.BlockSpec(tab, lambda i, h: (i % tiles, 0))],
        out_specs=[pl.BlockSpec(blk, lambda i, h: (i, h)), pl.BlockSpec(blk, lambda i, h: (i, h))],
        out_shape=[shape, shape],
        compiler_params=_cparams("parallel", "arbitrary"),
        name="rope_kv",
    )(p, p, cos_tab, sin_tab)


def _na_body(q_ref, kp_ref, kc_ref, kn_ref, vp_ref, vc_ref, vn_ref, kx_ref, vx_ref,
             cos_ref, sin_ref, bias_ref, o_ref):
    scale = HEAD_DIM ** -0.5
    lane = lax.broadcasted_iota(jnp.int32, (NA_TQ, HEAD_DIM), 1)
    first = (lane % (HEAD_DIM // 2)) < (HEAD_DIM // 4)
    q = q_ref[...]
    qr = _rope(q, cos_ref[...], sin_ref[...], first).astype(BF16)
    kr = jnp.concatenate([kp_ref[...], kc_ref[...], kn_ref[...]], axis=0)
    s_lat = lax.dot_general(qr, kr, _NT, preferred_element_type=F32) * scale + bias_ref[0, 0]
    s_ctx = lax.dot_general(q.astype(BF16), kx_ref[...].astype(BF16), _NT,
                            preferred_element_type=F32) * scale
    m = jnp.maximum(jnp.max(s_lat, axis=-1, keepdims=True), jnp.max(s_ctx, axis=-1, keepdims=True))
    p_lat = jnp.exp(s_lat - m)
    p_ctx = jnp.exp(s_ctx - m)
    denom = jnp.sum(p_lat, axis=-1, keepdims=True) + jnp.sum(p_ctx, axis=-1, keepdims=True)
    v_all = jnp.concatenate([vp_ref[...], vc_ref[...], vn_ref[...]], axis=0)
    o = (jnp.dot(p_lat.astype(BF16), v_all, preferred_element_type=F32)
         + jnp.dot(p_ctx.astype(BF16), vx_ref[...].astype(BF16), preferred_element_type=F32))
    o_ref[...] = (o / denom).astype(o_ref.dtype)


def neighbourhood_attention(p, pc, kr, vb, bias, cos_tab, sin_tab, *, batch, n_heads, col_q, ctx_len):
    n_tok = p.shape[0] // batch
    n_tiles = n_tok // NA_TQ
    col_k, col_v = col_q + n_heads, col_q + 2 * n_heads
    last = n_tiles - 1

    def tile(h, b, t):
        return b * n_tiles + t

    def prev(h, b, t):
        return b * n_tiles + jnp.maximum(t - 1, 0)

    def nxt(h, b, t):
        return b * n_tiles + jnp.minimum(t + 1, last)

    def variant(h, b, t):
        return jnp.where(t == 0, 0, jnp.where(t == last, 2, 1))

    blk = (NA_TQ, HEAD_DIM)
    ctx_blk = (ctx_len, HEAD_DIM)
    in_specs = [
        pl.BlockSpec(blk, lambda h, b, t: (tile(h, b, t), col_q + h)),
        pl.BlockSpec(blk, lambda h, b, t: (prev(h, b, t), h)),
        pl.BlockSpec(blk, lambda h, b, t: (tile(h, b, t), h)),
        pl.BlockSpec(blk, lambda h, b, t: (nxt(h, b, t), h)),
        pl.BlockSpec(blk, lambda h, b, t: (prev(h, b, t), h)),
        pl.BlockSpec(blk, lambda h, b, t: (tile(h, b, t), h)),
        pl.BlockSpec(blk, lambda h, b, t: (nxt(h, b, t), h)),
        pl.BlockSpec(ctx_blk, lambda h, b, t: (b, col_k + h)),
        pl.BlockSpec(ctx_blk, lambda h, b, t: (b, col_v + h)),
        pl.BlockSpec(blk, lambda h, b, t: (t, 0)),
        pl.BlockSpec(blk, lambda h, b, t: (t, 0)),
        pl.BlockSpec((1, 1, NA_TQ, 3 * NA_TQ), lambda h, b, t: (variant(h, b, t), h, 0, 0)),
    ]
    return pl.pallas_call(
        _na_body,
        grid=(n_heads, batch, n_tiles),
        in_specs=in_specs,
        out_specs=pl.BlockSpec(blk, lambda h, b, t: (tile(h, b, t), h)),
        out_shape=jax.ShapeDtypeStruct((p.shape[0], n_heads * HEAD_DIM), BF16),
        compiler_params=_cparams("parallel", "parallel", "arbitrary"),
        name="neighbourhood_attention",
    )(p, kr, kr, kr, vb, vb, vb, pc, pc, cos_tab, sin_tab, bias)


def _ctx_attn_body(q_ref, k_ref, v_ref, o_ref):
    nt = (((1,), (1,)), ((), ()))
    s = lax.dot_general(q_ref[...].astype(BF16), k_ref[...].astype(BF16), nt,
                        preferred_element_type=F32) * HEAD_DIM ** -0.5
    p = jnp.exp(s - jnp.max(s, axis=-1, keepdims=True))
    denom = jnp.sum(p, axis=-1, keepdims=True)
    o = jnp.dot(p.astype(BF16), v_ref[...].astype(BF16), preferred_element_type=F32)
    o_ref[...] = (o / denom).astype(o_ref.dtype)


def context_attention(pc, *, batch, n_heads, col_q, ctx_len):
    col_k, col_v = col_q + n_heads, col_q + 2 * n_heads
    blk = (ctx_len, HEAD_DIM)
    return pl.pallas_call(
        _ctx_attn_body,
        grid=(batch, n_heads),
        in_specs=[pl.BlockSpec(blk, lambda b, h: (b, col_q + h)),
                  pl.BlockSpec(blk, lambda b, h: (b, col_k + h)),
                  pl.BlockSpec(blk, lambda b, h: (b, col_v + h))],
        out_specs=pl.BlockSpec(blk, lambda b, h: (b, h)),
        out_shape=jax.ShapeDtypeStruct((pc.shape[0], n_heads * HEAD_DIM), BF16),
        compiler_params=_cparams("parallel", "parallel"),
        name="context_attention",
    )(pc, pc, pc)


def _conv_body(val_ref, gate_ref, pval_ref, pgate_ref, nval_ref, ngate_ref, w_ref, b_ref, lg_ref, lb_ref,
               o_ref, u_scr, *, tt):
    t = pl.program_id(1)

    def glu(v_ref, g_ref):
        return v_ref[...] * jax.nn.sigmoid(g_ref[...])

    span = tt + 2 * CONV_HALO
    u_scr[0, 0:CONV_HALO, :] = jnp.where(t > 0, glu(pval_ref, pgate_ref), 0.0)
    u_scr[0, CONV_HALO:CONV_HALO + tt, :] = glu(val_ref, gate_ref)
    u_scr[0, CONV_HALO + tt:span, :] = jnp.where(t < pl.num_programs(1) - 1, glu(nval_ref, ngate_ref), 0.0)
    u_scr[0, span:, :] = jnp.zeros((SUBLANES, u_scr.shape[2]), F32)
    for a in range(1, SUBLANES):
        u_scr[a, 0:span, :] = u_scr[0, a:a + span, :]
    off = CONV_HALO - CONV_K // 2

    def tap(r, k):
        start = r + off + k
        a = start % SUBLANES
        return w_ref[k:k + 1, :] * u_scr[a, start - a:start - a + CONV_ROWS, :]

    for r in range(0, tt, CONV_ROWS):
        acc = tap(r, 0)
        for k in range(1, CONV_K):
            acc = acc + tap(r, k)
        acc = acc + b_ref[...]
        mu = jnp.mean(acc, axis=-1, keepdims=True)
        cen = acc - mu
        var = jnp.mean(cen * cen, axis=-1, keepdims=True)
        un = cen * lax.rsqrt(var + NORM_EPS) * lg_ref[...] + lb_ref[...]
        o_ref[r:r + CONV_ROWS, :] = (un * jax.nn.sigmoid(un)).astype(o_ref.dtype)


def conformer_conv(p, w_dw, b_dw, ln_g, ln_b, *, batch, tt):
    C = w_dw.shape[1]
    T = p.shape[0] // batch
    nt = T // tt
    hb = tt // CONV_HALO
    seq_hb = T // CONV_HALO

    def prev_blk(b, t):
        return jnp.maximum(b * seq_hb + t * hb - 1, 0)

    def next_blk(b, t):
        return jnp.minimum(b * seq_hb + (t + 1) * hb, batch * seq_hb - 1)

    main = (tt, C)
    halo = (CONV_HALO, C)
    vec = pl.BlockSpec((1, C), lambda b, t: (0, 0))
    return pl.pallas_call(
        functools.partial(_conv_body, tt=tt),
        grid=(batch, nt),
        in_specs=[pl.BlockSpec(main, lambda b, t: (b * nt + t, 0)),
                  pl.BlockSpec(main, lambda b, t: (b * nt + t, 1)),
                  pl.BlockSpec(halo, lambda b, t: (prev_blk(b, t), 0)),
                  pl.BlockSpec(halo, lambda b, t: (prev_blk(b, t), 1)),
                  pl.BlockSpec(halo, lambda b, t: (next_blk(b, t), 0)),
                  pl.BlockSpec(halo, lambda b, t: (next_blk(b, t), 1)),
                  pl.BlockSpec((CONV_K, C), lambda b, t: (0, 0)), vec, vec, vec],
        out_specs=pl.BlockSpec(main, lambda b, t: (b * nt + t, 0)),
        out_shape=jax.ShapeDtypeStruct((p.shape[0], C), BF16),
        scratch_shapes=[pltpu.VMEM((SUBLANES, tt + 2 * CONV_HALO + SUBLANES, C), F32)],
        compiler_params=_cparams("parallel", "parallel"),
        name="conformer_conv",
    )(p, p, p, p, p, p, w_dw, b_dw, ln_g, ln_b)


_NT = (((1,), (1,)), ((), ()))
_TN = (((0,), (0,)), ((), ()))


def _hg_gates(zq, zf, lb):
    q = zq * jax.nn.sigmoid(zq) * HEAD_DIM ** -0.5
    f = lb + (1.0 - lb) * jax.nn.sigmoid(zf)
    log_f = jnp.log(jnp.maximum(f, F_MIN))
    k = (1.0 - lb) * jax.nn.sigmoid(-zf)
    return q, k, log_f


def _block_cumsum(tri, g):
    out = []
    for r in range(g.shape[0] // tri.shape[0]):
        x = g[r * tri.shape[0]:(r + 1) * tri.shape[0], :]
        hi = x.astype(BF16)
        rest = x - hi.astype(F32)
        mid = rest.astype(BF16)
        lo = (rest - mid.astype(F32)).astype(BF16)
        out.append(jnp.dot(tri, hi, preferred_element_type=F32)
                   + jnp.dot(tri, mid, preferred_element_type=F32)
                   + jnp.dot(tri, lo, preferred_element_type=F32))
    return jnp.concatenate(out, axis=0)


def _hg_factors(d, r, L, q_s, k_s, b_s, fac_s):
    rows = slice(r, r + L)
    mid = r + L // 2 - 1
    end = r if d == 1 else r + L - 1
    b, q, k = b_s[d, rows, :], q_s[d, rows, :], k_s[d, rows, :]
    bm = b - b_s[d, mid:mid + 1, :]
    fac_s[0, d, rows, :] = (q * jnp.exp(bm)).astype(BF16)
    fac_s[1, d, rows, :] = (k * jnp.exp(-bm)).astype(BF16)
    fac_s[2, d, rows, :] = (q * jnp.exp(b)).astype(BF16)
    fac_s[3, d, rows, :] = (k * jnp.exp(b_s[d, end:end + 1, :] - b)).astype(BF16)


def _hg_chunk(r, d, st, fac_s, b_s, v_ref, o_ref, reverse, L):
    rows = slice(r, r + L)
    qa, ka, qi, kd = (fac_s[i, d, rows, :] for i in range(4))
    vI = v_ref[rows, :].astype(BF16)
    att = lax.dot_general(qa, ka, _NT, preferred_element_type=F32)
    ti = lax.broadcasted_iota(jnp.int32, (L, L), 0)
    si = lax.broadcasted_iota(jnp.int32, (L, L), 1)
    att = jnp.where((si >= ti) if reverse else (si <= ti), att, 0.0)
    o_ref[rows, :] = (jnp.dot(att.astype(BF16), vI, preferred_element_type=F32)
                      + lax.dot_general(qi, st.astype(BF16), _NT, preferred_element_type=F32))
    end = r if reverse else r + L - 1
    upd = lax.dot_general(vI, kd, _TN, preferred_element_type=F32)
    return st * jnp.exp(b_s[d, end:end + 1, :]) + upd


def _hg_sub_block(r, d, q_s, k_s, b_s, v_ref, o_ref, st_s, reverse):
    nt, tn = _NT, _TN
    bI = b_s[d, pl.ds(r, HG_SUB), :]
    qI = q_s[d, pl.ds(r, HG_SUB), :]
    kI = k_s[d, pl.ds(r, HG_SUB), :]
    vI = v_ref[pl.ds(r, HG_SUB), :]
    st = st_s[d]
    o_inter = lax.dot_general((qI * jnp.exp(bI)).astype(BF16), st.astype(BF16), nt,
                              preferred_element_type=F32)
    sub = lax.broadcasted_iota(jnp.int32, (HG_SUB, HEAD_DIM), 0)
    slabs = []
    for t in range(HG_SUB):
        keep = (sub >= t) if reverse else (sub <= t)
        decay = jnp.where(keep, jnp.exp(bI[t:t + 1, :] - bI), 0.0)
        slabs.append(decay * (qI[t:t + 1, :] * kI))
    x3 = jnp.concatenate(slabs, axis=0).astype(BF16)
    att = jnp.dot(x3, jnp.ones((HEAD_DIM, HEAD_DIM), BF16), preferred_element_type=F32)
    z = att * jnp.concatenate([vI] * HG_SUB, axis=0)
    o_intra = jnp.sum(z.reshape(HG_SUB, HG_SUB, HEAD_DIM), axis=1)
    o_ref[pl.ds(r, HG_SUB), :] = o_inter + o_intra
    b_end = bI[0:1, :] if reverse else bI[HG_SUB - 1:HG_SUB, :]
    k_dec = (kI * jnp.exp(b_end - bI)).astype(BF16)
    upd = lax.dot_general(vI.astype(BF16), k_dec, tn, preferred_element_type=F32)
    st_s[d] = st * jnp.exp(b_end) + upd


def _hg_body(zqf_ref, zvf_ref, zff_ref, zqb_ref, zvb_ref, zfb_ref, lbf_ref, lbb_ref,
             tri_ref, s0_ref, of_ref, ob_ref, sout_ref,
             q_s, k_s, g_s, b_s, fac_s, st_s, *, tb):
    c = pl.program_id(1)

    @pl.when(c == 0)
    def _():
        st_s[...] = s0_ref[...]

    for d, (zq_ref, zf_ref, lb_ref) in enumerate(((zqf_ref, zff_ref, lbf_ref), (zqb_ref, zfb_ref, lbb_ref))):
        q_s[d], k_s[d], g_s[d] = _hg_gates(zq_ref[...], zf_ref[...], lb_ref[...])

    def cumulate(level):
        for d in range(2):
            b_s[d] = _block_cumsum(tri_ref[2 * level + d], g_s[d])

    def spread(L):
        dev = None
        for d in range(2):
            for r in range(0, tb, L):
                mid = r + L // 2 - 1
                dev_r = jnp.abs(b_s[d, r:r + L, :] - b_s[d, mid:mid + 1, :])
                dev = dev_r if dev is None else jnp.maximum(dev, dev_r)
        return jnp.max(dev)

    def factored(L):
        for d in range(2):
            for r in range(0, tb, L):
                _hg_factors(d, r, L, q_s, k_s, b_s, fac_s)
        n = tb // L
        st_f, st_b = st_s[0], st_s[1]
        for i in range(n):
            st_f = _hg_chunk(i * L, 0, st_f, fac_s, b_s, zvf_ref, of_ref, False, L)
            st_b = _hg_chunk((n - 1 - i) * L, 1, st_b, fac_s, b_s, zvb_ref, ob_ref, True, L)
        st_s[0] = st_f
        st_s[1] = st_b

    def dispatch(level):
        cumulate(level)
        if level == len(HG_CHUNKS):
            exact()
            return
        safe = spread(HG_CHUNKS[level]) <= HG_SAFE_DECAY
        pl.when(safe)(lambda: factored(HG_CHUNKS[level]))
        pl.when(jnp.logical_not(safe))(lambda: dispatch(level + 1))

    def exact():
        n = tb // HG_SUB

        def step(i, carry):
            rf = pl.multiple_of(i * HG_SUB, HG_SUB)
            _hg_sub_block(rf, 0, q_s, k_s, b_s, zvf_ref, of_ref, st_s, False)
            rb = pl.multiple_of((n - 1 - i) * HG_SUB, HG_SUB)
            _hg_sub_block(rb, 1, q_s, k_s, b_s, zvb_ref, ob_ref, st_s, True)
            return carry

        lax.fori_loop(0, n, step, 0)

    dispatch(0)

    @pl.when(c == pl.num_programs(1) - 1)
    def _():
        sout_ref[...] = st_s[...]


def hgrn2_bidir(p, lb_f, lb_b, s0, *, batch, n_heads, col0, tb):
    n_tok = p.shape[0] // batch
    nb = n_tok // tb
    H = n_heads
    tri = []
    for size in HG_CHUNKS + (HG_SUB,):
        eye = np.kron(np.eye(HG_TRI // size), np.ones((size, size)))
        tri += [np.tril(eye), np.triu(eye)]
    tri = jnp.asarray(np.stack(tri), BF16)

    def fwd(col):
        return pl.BlockSpec((tb, HEAD_DIM), lambda bh, c: ((bh // H) * nb + c, col + bh % H))

    def bwd(col):
        return pl.BlockSpec((tb, HEAD_DIM), lambda bh, c: ((bh // H) * nb + nb - 1 - c, col + bh % H))

    def out(rev):
        if rev:
            return pl.BlockSpec((tb, HEAD_DIM), lambda bh, c: ((bh // H) * nb + nb - 1 - c, bh % H))
        return pl.BlockSpec((tb, HEAD_DIM), lambda bh, c: ((bh // H) * nb + c, bh % H))

    lb_spec = pl.BlockSpec((None, 1, HEAD_DIM), lambda bh, c: (bh % H, 0, 0))
    tri_spec = pl.BlockSpec(tri.shape, lambda bh, c: (0, 0, 0))
    st_spec = pl.BlockSpec((None, 2, HEAD_DIM, HEAD_DIM), lambda bh, c: (bh, 0, 0, 0))
    o_shape = jax.ShapeDtypeStruct((p.shape[0], H * HEAD_DIM), F32)
    dir_buf = pltpu.VMEM((2, tb, HEAD_DIM), F32)
    return pl.pallas_call(
        functools.partial(_hg_body, tb=tb),
        grid=(batch * H, nb),
        in_specs=[fwd(col0), fwd(col0 + H), fwd(col0 + 2 * H),
                  bwd(col0), bwd(col0 + H), bwd(col0 + 3 * H),
                  lb_spec, lb_spec, tri_spec, st_spec],
        out_specs=[out(False), out(True), st_spec],
        out_shape=[o_shape, o_shape, jax.ShapeDtypeStruct(s0.shape, F32)],
        scratch_shapes=[dir_buf, dir_buf, dir_buf, dir_buf, pltpu.VMEM((4, 2, tb, HEAD_DIM), BF16),
                        pltpu.VMEM((2, HEAD_DIM, HEAD_DIM), F32)],
        compiler_params=_cparams("parallel", "arbitrary"),
        name="hgrn2_bidir",
    )(p, p, p, p, p, p, lb_f, lb_b, tri, s0)


def _hg_out_body(of_ref, ob_ref, g_ref, gain_ref, y_ref):
    for h in range(of_ref.shape[1] // HEAD_DIM):
        cols = slice(h * HEAD_DIM, (h + 1) * HEAD_DIM)
        o = of_ref[:, cols] + ob_ref[:, cols]
        on = o * lax.rsqrt(jnp.mean(o * o, axis=-1, keepdims=True) + NORM_EPS) * gain_ref[:, cols]
        g = g_ref[:, cols]
        y_ref[:, cols] = (on * (g * jax.nn.sigmoid(g))).astype(y_ref.dtype)


def hgrn2_output(o_f, o_b, p, gain, *, n_heads, col_g, tm):
    M = o_f.shape[0]
    grp = _head_group(n_heads, col_g)
    blk = (tm, grp * HEAD_DIM)
    return pl.pallas_call(
        _hg_out_body,
        grid=(M // tm, n_heads // grp),
        in_specs=[pl.BlockSpec(blk, lambda i, h: (i, h)),
                  pl.BlockSpec(blk, lambda i, h: (i, h)),
                  pl.BlockSpec(blk, lambda i, h: (i, col_g // grp + h)),
                  pl.BlockSpec((1, grp * HEAD_DIM), lambda i, h: (0, h))],
        out_specs=pl.BlockSpec(blk, lambda i, h: (i, h)),
        out_shape=jax.ShapeDtypeStruct(o_f.shape, BF16),
        compiler_params=_cparams("parallel", "parallel"),
        name="hgrn2_output",
    )(o_f, o_b, p, gain)


def _ffn_body(x_ref, wg_ref, wu_ref, wd_ref, gate_ref, res_gate_ref, o_ref):
    x = x_ref[...]
    u = jnp.dot(x, wg_ref[...], preferred_element_type=F32)
    w = jnp.dot(x, wu_ref[...], preferred_element_type=F32)
    a = (u * jax.nn.sigmoid(u) * w).astype(BF16)
    o_ref[...] = jnp.dot(a, wd_ref[...], preferred_element_type=F32) * gate_ref[...] * res_gate_ref[0]


def expert_ffn(xe, w_gate, w_up, w_down, layer, gate, res_gate, *, tm):
    B, E, cap, D = xe.shape
    F = w_gate.shape[-1]
    per_sample = res_gate.shape[0] > 1
    return pl.pallas_call(
        _ffn_body,
        grid=(E, B, cap // tm),
        in_specs=[pl.BlockSpec((None, None, tm, D), lambda e, b, r: (b, e, r, 0)),
                  pl.BlockSpec((None, None, D, F), lambda e, b, r: (layer, e, 0, 0)),
                  pl.BlockSpec((None, None, D, F), lambda e, b, r: (layer, e, 0, 0)),
                  pl.BlockSpec((None, None, F, D), lambda e, b, r: (layer, e, 0, 0)),
                  pl.BlockSpec((None, None, tm, 1), lambda e, b, r: (b, e, r, 0)),
                  pl.BlockSpec((1, 1, D), lambda e, b, r: (b if per_sample else 0, 0, 0))],
        out_specs=pl.BlockSpec((None, None, tm, D), lambda e, b, r: (b, e, r, 0)),
        out_shape=jax.ShapeDtypeStruct((B, E, cap, D), F32),
        compiler_params=_cparams("parallel", "parallel", "arbitrary"),
        name="expert_ffn",
    )(xe, w_gate, w_up, w_down, gate, res_gate)


def _expert_choice_ffn(x, h, logits, res_gate, w_gate, w_up, w_down, layer):
    B, n, D = x.shape
    cap = EC_CAPACITY * n // N_EXPERTS
    aff = jax.nn.softmax(logits, axis=-1)
    gate, idx = lax.top_k(aff.transpose(0, 2, 1), cap)
    bi = jnp.arange(B)[:, None, None]
    xe = h[bi, idx]
    y = expert_ffn(xe, w_gate, w_up, w_down, layer, gate[..., None], res_gate, tm=min(cap, 512))
    return x.at[bi, idx].add(y)


def kernel(x, c, ctx, c_ctx, mod_down, mod_up, mod_bias, mix_norm, ffn_norm, w_in, w_out,
           conv_w, conv_b, conv_ln_g, conv_ln_b, na_rpb, hg_lb_logits, hg_norm, router,
           w_gate, w_up, w_down, final_norm):
    B, N, D = x.shape
    M = ctx.shape[1]
    depth = w_in.shape[0]
    rows = N // GRID_W
    w_conv = conv_w.shape[-1]
    w_hg = hg_norm.shape[-1]
    n_hg = w_hg // HEAD_DIM
    n_na = na_rpb.shape[1]
    w_na = n_na * HEAD_DIM
    in_conv = 2 * w_conv
    col_na = in_conv // HEAD_DIM
    col_hg = (in_conv + 3 * w_na) // HEAD_DIM
    n_exp = router.shape[-1]

    lb_p = jax.nn.softmax(hg_lb_logits.astype(F32), axis=1)
    lb_all = jnp.cumsum(lb_p, axis=1) - lb_p[:, :1]
    a_lat = jax.nn.silu(c)
    a_ctx = jax.nn.silu(c_ctx)
    cos_tab, sin_tab = _rope_tables(N)
    s_zero = jnp.zeros((B * n_hg, 2, HEAD_DIM, HEAD_DIM), F32)
    experts = (w_gate.astype(BF16), w_up.astype(BF16), w_down.astype(BF16))

    xl = x.reshape(B * N, D)
    xc = ctx.reshape(B * M, D)
    for l in range(depth):
        last = l == depth - 1
        hi = lax.Precision.HIGHEST
        mod = jnp.dot(jnp.dot(a_lat, mod_down[l], precision=hi), mod_up[l], precision=hi) + mod_bias[l]
        mod_c = jnp.dot(jnp.dot(a_ctx, mod_down[l], precision=hi), mod_up[l], precision=hi) + mod_bias[l]
        sh1, sc1, g1, sh2, sc2, g2 = [m[:, None, :] for m in jnp.split(mod, N_MOD, axis=-1)]
        csh1, csc1, cg1, csh2, csc2, cg2 = [m[None, None, :] for m in jnp.split(mod_c, N_MOD, axis=-1)]

        g_mix = mix_norm[l][None, :]
        g_ffn = ffn_norm[l][None, :]
        router_f = jnp.pad(router[l], ((0, 0), (0, ROUTER_PAD - n_exp)))
        router_hi = router_f.astype(BF16)
        router_l = jnp.stack([router_hi, (router_f - router_hi.astype(F32)).astype(BF16)])
        conv_args = (conv_w[l], conv_b[l][None, :], conv_ln_g[l][None, :], conv_ln_b[l][None, :])

        tn_in = _tile(w_in.shape[2], 512)
        tn_out = _tile(D, 512)
        tm_lat = _tile(N, 1024)
        h1 = norm_mod(xl, g_mix, sc1, sh1, tm=512)
        hc1 = norm_mod(xc, g_mix, csc1, csh1, tm=B * M)
        p = matmul_wcast(h1, w_in, l, tm=tm_lat, tn=tn_in)
        pc = matmul_wcast(hc1, w_in, l, tm=B * M, tn=tn_in)

        ya = conformer_conv(p, *conv_args, batch=B, tt=_tile(N, 512))
        bias = _na_bias(na_rpb[l], rows)
        kr, vb = rope_kv(p, cos_tab, sin_tab, batch=B, n_heads=n_na, col_k=col_na + n_na, tq=_tile(N, 512))
        yb = neighbourhood_attention(p, pc, kr, vb, bias, cos_tab, sin_tab, batch=B, n_heads=n_na,
                                     col_q=col_na, ctx_len=M)
        lb_f = lb_all[0, l].reshape(n_hg, 1, HEAD_DIM)
        lb_b = lb_all[1, l].reshape(n_hg, 1, HEAD_DIM)
        oc_f, oc_b, s_ctx = hgrn2_bidir(pc, lb_f, lb_b, s_zero, batch=B, n_heads=n_hg, col0=col_hg, tb=M)
        o_f, o_b, _ = hgrn2_bidir(p, lb_f, lb_b, s_ctx, batch=B, n_heads=n_hg, col0=col_hg, tb=HG_TB)
        gain = hg_norm[l][None, :]
        yc = hgrn2_output(o_f, o_b, p, gain, n_heads=n_hg, col_g=col_hg + 4 * n_hg, tm=512)

        xl = outproj_residual(ya, yb, yc, w_out, l, xl, g1, tm=tm_lat, tn=tn_out)
        h2, logits = norm_mod(xl, g_ffn, sc2, sh2, tm=512, router=router_l)
        xl = _expert_choice_ffn(xl.reshape(B, N, D), h2.reshape(B, N, D),
                                logits[:, :n_exp].reshape(B, N, n_exp), g2, *experts, l).reshape(B * N, D)

        if not last:
            yac = conformer_conv(pc, *conv_args, batch=B, tt=M)
            ybc = context_attention(pc, batch=B, n_heads=n_na, col_q=col_na, ctx_len=M)
            ycc = hgrn2_output(oc_f, oc_b, pc, gain, n_heads=n_hg, col_g=col_hg + 4 * n_hg, tm=B * M)
            xc = outproj_residual(yac, ybc, ycc, w_out, l, xc, cg1, tm=B * M, tn=tn_out)
            hc2, logits_c = norm_mod(xc, g_ffn, csc2, csh2, tm=B * M, router=router_l)
            xc = _expert_choice_ffn(xc.reshape(B, M, D), hc2.reshape(B, M, D),
                                    logits_c[:, :n_exp].reshape(B, M, n_exp), cg2, *experts, l).reshape(B * M, D)

    zero = jnp.zeros((1, 1, D), F32)
    out = norm_mod(xl, final_norm[None, :], zero, zero, tm=512, out_dtype=F32)
    return out.reshape(B, N, D)
```
